```python
import math
import jax, jax.numpy as jnp
from jax import lax
import numpy as np


D_MODEL = 1024
BATCH = 32
SEQ = 2048
DEPTH = 4

CHUNK = 64
Q_BLOCK = 128
MIX_WIDTH = D_MODEL // 2
N_BRANCH = 3
DA_QK_DIM = 64
DA_V_DIM = 2 * DA_QK_DIM
DA_HEADS = MIX_WIDTH // DA_V_DIM
HG_EXPAND = 128
HG_HEADS = MIX_WIDTH // HG_EXPAND
HG_V_DIM = MIX_WIDTH // HG_HEADS
SB_DIM = 64
SB_HEADS = MIX_WIDTH // SB_DIM
REL_BUCKETS = 32
REL_MAX_DIST = 128
FFN_HIDDEN = -(-8 * D_MODEL // (3 * 256)) * 256
N_IN_SLICES = 10
IN_WIDTH = N_IN_SLICES * MIX_WIDTH + N_BRANCH * D_MODEL
NORM_EPS = 1e-6

kernel_name = 'hybrid_gated_diffattn_hgrn2_stickbreaking'


def rms_norm(x, g):
    xf = x.astype(jnp.float32)
    y = xf * lax.rsqrt(jnp.mean(xf * xf, axis=-1, keepdims=True) + NORM_EPS)
    return (y * g.astype(jnp.float32)).astype(x.dtype)


def t5_bucket(rel):
    half = REL_BUCKETS // 2
    max_exact = half // 2
    ret = (rel > 0).astype(jnp.int32) * half
    n = jnp.abs(rel)
    nf = jnp.maximum(n, 1).astype(jnp.float32)
    large = max_exact + (jnp.log(nf / max_exact) / math.log(REL_MAX_DIST / max_exact)
                         * (half - max_exact)).astype(jnp.int32)
    large = jnp.minimum(large, half - 1)
    return ret + jnp.where(n < max_exact, n, large)


def diff_attention(q, k, v, lam, lam_init, subln_g, rel_bias):
    B, S = q.shape[0], q.shape[1]
    pos = jnp.arange(S, dtype=jnp.int32)
    chunk_id = pos // CHUNK
    scale = DA_QK_DIM ** -0.5
    outs = []
    for q0 in range(0, S, Q_BLOCK):
        kl = q0 + Q_BLOCK
        logits = jnp.einsum('bqhcd,bkhcd->bhcqk', q[:, q0:kl], k[:, :kl]).astype(jnp.float32) * scale
        bias = rel_bias[t5_bucket(pos[None, :kl] - pos[q0:kl, None])]
        logits = logits + jnp.transpose(bias, (2, 0, 1)).astype(jnp.float32)[None, :, None]
        mask = chunk_id[None, :kl] <= chunk_id[q0:kl, None]
        p = jax.nn.softmax(jnp.where(mask, logits, -jnp.inf), axis=-1)
        attn = p[:, :, 0] - lam * p[:, :, 1]
        outs.append(jnp.einsum('bhqk,bkhd->bqhd', attn.astype(v.dtype), v[:, :kl]))
    o = jnp.concatenate(outs, axis=1)
    o = rms_norm(o, subln_g) * (1.0 - lam_init)
    return o.reshape(B, S, DA_HEADS * DA_V_DIM)


def hgrn2(f_pre, i, q, g, lb, norm_g):
    B, S, H, DK = f_pre.shape
    DV = i.shape[-1]
    f32 = jnp.float32
    lb = lb.reshape(H, DK)
    f = lb + (1.0 - lb) * jax.nn.sigmoid(f_pre.astype(f32))
    log_f = jnp.log(f)
    kgate = 1.0 - f
    qf = jax.nn.silu(q.astype(f32))
    vf = i.astype(f32)
    n = S // CHUNK

    def to_chunks(t):
        return t.reshape(B, n, CHUNK, H, t.shape[-1]).swapaxes(0, 1)

    incl = jnp.tril(jnp.ones((CHUNK, CHUNK), dtype=bool))

    def step(state, inp):
        qc, kc, vc, lfc = inp
        cum = jnp.cumsum(lfc, axis=1)
        rel = cum[:, :, None] - cum[:, None, :]
        decay = jnp.exp(jnp.where(incl[None, :, :, None, None], rel, -jnp.inf))
        scores = jnp.einsum('bthd,bshd,btshd->bhts', qc, kc, decay)
        intra = jnp.einsum('bhts,bshv->bthv', scores, vc)
        inter = jnp.einsum('bthd,bhdv->bthv', qc * jnp.exp(cum), state)
        last = cum[:, -1]
        k_dec = kc * jnp.exp(last[:, None] - cum)
        state = state * jnp.exp(last)[..., None] + jnp.einsum('bshd,bshv->bhdv', k_dec, vc)
        return state, intra + inter

    state0 = jnp.zeros((B, H, DK, DV), f32)
    _, o = lax.scan(step, state0, (to_chunks(qf), to_chunks(kgate), to_chunks(vf), to_chunks(log_f)))
    o = o.swapaxes(0, 1).reshape(B, S, H, DV)
    o = rms_norm(o, norm_g) * jax.nn.silu(g.astype(f32))
    return o.astype(i.dtype).reshape(B, S, H * DV)


def stick_breaking(q, k, v):
    B, S = q.shape[0], q.shape[1]
    pos = jnp.arange(S, dtype=jnp.int32)
    scale = SB_DIM ** -0.5
    outs = []
    for q0 in range(0, S, Q_BLOCK):
        kl = q0 + Q_BLOCK
        z = jnp.einsum('bqhd,bkhd->bhqk', q[:, q0:kl], k[:, :kl]).astype(jnp.float32) * scale
        strict = pos[None, :kl] < pos[q0:kl, None]
        log_keep = jnp.where(strict, jax.nn.log_sigmoid(-z), 0.0)
        later = lax.cumsum(log_keep, axis=3, reverse=True) - log_keep
        w = jnp.where(strict, jnp.exp(jax.nn.log_sigmoid(z) + later), 0.0)
        outs.append(jnp.einsum('bhqk,bkhd->bqhd', w.astype(v.dtype), v[:, :kl]))
    o = jnp.concatenate(outs, axis=1)
    return o.reshape(B, S, SB_HEADS * SB_DIM)


def setup_inputs(seed: int = 0) -> dict:
    key = jax.random.key(seed)
    ks = jax.random.split(key, 14)
    nrm = jax.random.normal
    f32 = jnp.float32
    return {
        'x': nrm(ks[0], (BATCH, SEQ, D_MODEL), f32),
        'norm_mix_g': 1.0 + 0.02 * nrm(ks[1], (DEPTH, D_MODEL), f32),
        'w_in': nrm(ks[2], (DEPTH, D_MODEL, IN_WIDTH), f32) * D_MODEL ** -0.5,
        'rel_bias': 0.5 * nrm(ks[3], (REL_BUCKETS, DA_HEADS), f32),
        'diff_lambda': 0.1 * nrm(ks[4], (DEPTH, 4, DA_QK_DIM), f32),
        'diff_subln_g': 1.0 + 0.02 * nrm(ks[5], (DEPTH, DA_V_DIM), f32),
        'hgrn_lb_logits': 0.5 * nrm(ks[6], (DEPTH, HG_HEADS * HG_EXPAND), f32),
        'hgrn_norm_g': 1.0 + 0.02 * nrm(ks[7], (DEPTH, HG_V_DIM), f32),
        'w_up': nrm(ks[8], (DEPTH, N_BRANCH, MIX_WIDTH, D_MODEL), f32) * MIX_WIDTH ** -0.5,
        'w_out': nrm(ks[9], (DEPTH, D_MODEL, D_MODEL), f32) * D_MODEL ** -0.5,
        'norm_ffn_g': 1.0 + 0.02 * nrm(ks[10], (DEPTH, D_MODEL), f32),
        'w_ffn_in': nrm(ks[11], (DEPTH, D_MODEL, 2 * FFN_HIDDEN), f32) * D_MODEL ** -0.5,
        'w_ffn_out': nrm(ks[12], (DEPTH, FFN_HIDDEN, D_MODEL), f32) * FFN_HIDDEN ** -0.5,
        'final_norm_g': 1.0 + 0.02 * nrm(ks[13], (D_MODEL,), f32),
    }


def reference(x, norm_mix_g, w_in, rel_bias, diff_lambda, diff_subln_g, hgrn_lb_logits,
              hgrn_norm_g, w_up, w_out, norm_ffn_g, w_ffn_in, w_ffn_out, final_norm_g):
    f32 = jnp.float32
    B, S, _ = x.shape
    lb_all = jnp.cumsum(jax.nn.softmax(hgrn_lb_logits.astype(f32), axis=0), axis=0)
    split_points = [MIX_WIDTH * j for j in range(1, N_IN_SLICES + 1)]
    for l in range(DEPTH):
        h = rms_norm(x, norm_mix_g[l])
        proj = h @ w_in[l]
        (da_q, da_k, da_v, hg_f, hg_i, hg_q, hg_g,
         sb_q, sb_k, sb_v, gate_pre) = jnp.split(proj, split_points, axis=-1)
        lam_init = 0.8 - 0.6 * math.exp(-0.3 * l)
        lq1, lk1, lq2, lk2 = diff_lambda[l].astype(f32)
        lam = jnp.exp(jnp.sum(lq1 * lk1)) - jnp.exp(jnp.sum(lq2 * lk2)) + lam_init
        y_da = diff_attention(da_q.reshape(B, S, DA_HEADS, 2, DA_QK_DIM),
                              da_k.reshape(B, S, DA_HEADS, 2, DA_QK_DIM),
                              da_v.reshape(B, S, DA_HEADS, DA_V_DIM),
                              lam, lam_init, diff_subln_g[l], rel_bias)
        lb = lb_all[l] - lb_all[0]
        y_hg = hgrn2(hg_f.reshape(B, S, HG_HEADS, HG_EXPAND),
                     hg_i.reshape(B, S, HG_HEADS, HG_V_DIM),
                     hg_q.reshape(B, S, HG_HEADS, HG_EXPAND),
                     hg_g.reshape(B, S, HG_HEADS, HG_V_DIM),
                     lb, hgrn_norm_g[l])
        y_sb = stick_breaking(sb_q.reshape(B, S, SB_HEADS, SB_DIM),
                              sb_k.reshape(B, S, SB_HEADS, SB_DIM),
                              sb_v.reshape(B, S, SB_HEADS, SB_DIM))
        branches = jnp.stack([y_da, y_hg, y_sb], axis=2)
        up = jnp.einsum('bsnw,nwd->bsnd', branches, w_up[l])
        gates = jax.nn.sigmoid(gate_pre.reshape(B, S, N_BRANCH, D_MODEL))
        x = x + jnp.sum(gates * up, axis=2) @ w_out[l]
        h = rms_norm(x, norm_ffn_g[l])
        a, b = jnp.split(h @ w_ffn_in[l], 2, axis=-1)
        x = x + (jax.nn.silu(a) * b) @ w_ffn_out[l]
    return rms_norm(x, final_norm_g)
```

```python
import functools
import math

import numpy as np
import jax
import jax.numpy as jnp
from jax import lax
from jax.experimental import pallas as pl
from jax.experimental.pallas import tpu as pltpu

F32 = jnp.float32
BF16 = jnp.bfloat16

D_MODEL = 1024
MIX_WIDTH = D_MODEL // 2
N_BRANCH = 3
DA_QK_DIM = 64
DA_HEADS = 4
HG_HEADS = 4
SB_DIM = 64
SB_HEADS = 8
CHUNK = 64
REL_BUCKETS = 32
REL_MAX_DIST = 128
FFN_HIDDEN = 2816
IN_WIDTH = 10 * MIX_WIDTH + N_BRANCH * D_MODEL
NORM_EPS = 1e-6

LANES = 128
SUBLANES = 8
COL_DA_Q, COL_DA_K, COL_DA_V = 0, 4, 8
COL_HG_F, COL_HG_I, COL_HG_Q, COL_HG_G = 12, 16, 20, 24
COL_SB_Q, COL_SB_K, COL_SB_V = 28, 32, 36
COL_GATE = 40

ATT_TILE = 256
HG_BLOCK = 128
HG_SUB = 16
FFN_CHUNK = 256
VMEM_LIMIT = 56 * 1024 * 1024


def _cparams(n_axes, vmem=None):
    return pltpu.CompilerParams(
        dimension_semantics=("arbitrary",) * n_axes,
        vmem_limit_bytes=vmem,
    )


def _rms(x, g):
    ms = jnp.mean(x * x, axis=-1, keepdims=True)
    return x * lax.rsqrt(ms + NORM_EPS) * g


def _sigmoid(x):
    return 1.0 / (1.0 + jnp.exp(-x))


def _inproj_kernel(x_ref, g_ref, w_ref, o_ref, h_ref):
    @pl.when(pl.program_id(1) == 0)
    def _():
        h_ref[...] = _rms(x_ref[...], g_ref[...]).astype(BF16)

    o_ref[...] = jnp.dot(h_ref[...], w_ref[...], preferred_element_type=F32).astype(o_ref.dtype)


def _inproj(x2d, g, w, tm=1024, tn=1024):
    m, d = x2d.shape
    n = w.shape[1]
    return pl.pallas_call(
        _inproj_kernel,
        grid=(m // tm, n // tn),
        in_specs=[
            pl.BlockSpec((tm, d), lambda i, j: (i, 0)),
            pl.BlockSpec((1, d), lambda i, j: (0, 0)),
            pl.BlockSpec((d, tn), lambda i, j: (0, j)),
        ],
        out_specs=pl.BlockSpec((tm, tn), lambda i, j: (i, j)),
        out_shape=jax.ShapeDtypeStruct((m, n), BF16),
        scratch_shapes=[pltpu.VMEM((tm, d), BF16)],
        compiler_params=_cparams(2, VMEM_LIMIT),
        name="inproj",
    )(x2d, g, w)


def _da_kernel(q_ref, k_ref, v_ref, bias_ref, lam_ref, g_ref, o_ref, *, T):
    qi = pl.program_id(2)
    lane = lax.broadcasted_iota(jnp.int32, (T, LANES), 1)
    qs = q_ref[0] * jnp.asarray(DA_QK_DIM ** -0.5, BF16)
    zero = jnp.zeros_like(qs)
    qmaps = (jnp.where(lane < DA_QK_DIM, qs, zero), jnp.where(lane >= DA_QK_DIM, qs, zero))
    row = lax.broadcasted_iota(jnp.int32, (T, T), 0)
    col = lax.broadcasted_iota(jnp.int32, (T, T), 1)
    chunk_mask = (col // CHUNK) <= (row // CHUNK)

    def tile(j, carry, diag):
        r0 = pl.multiple_of(j * T, T)
        kt = k_ref[0, pl.ds(r0, T), :]
        vt = v_ref[0, pl.ds(r0, T), :]
        bias = bias_ref[0, jnp.minimum(qi - j, 2)]
        out = []
        for c in range(2):
            m, l, acc = carry[c]
            s = lax.dot_general(qmaps[c], kt, (((1,), (1,)), ((), ())),
                                preferred_element_type=F32) + bias
            if diag:
                s = jnp.where(chunk_mask, s, -jnp.inf)
            m_new = jnp.maximum(m, jnp.max(s, axis=-1, keepdims=True))
            alpha = jnp.exp(m - m_new)
            p = jnp.exp(s - m_new)
            l = alpha * l + jnp.sum(p, axis=-1, keepdims=True)
            acc = alpha * acc + jnp.dot(p.astype(BF16), vt, preferred_element_type=F32)
            out.append((m_new, l, acc))
        return tuple(out)

    init = tuple((jnp.full((T, 1), -jnp.inf, F32), jnp.zeros((T, 1), F32),
                  jnp.zeros((T, LANES), F32)) for _ in range(2))
    carry = tile(qi, init, True)
    carry = lax.fori_loop(0, qi, lambda j, c: tile(j, c, False), carry)
    (_, l0, a0), (_, l1, a1) = carry
    o = a0 / l0 - lam_ref[...] * (a1 / l1)
    o_ref[0] = _rms(o, g_ref[...]).astype(o_ref.dtype)


def _diff_attn(proj3, bias_tbl, lam_row, g_row, T=ATT_TILE):
    b, s, _ = proj3.shape
    kern = functools.partial(_da_kernel, T=T)
    return pl.pallas_call(
        kern,
        grid=(b, DA_HEADS, s // T),
        in_specs=[
            pl.BlockSpec((1, T, LANES), lambda bi, h, qi: (bi, qi, COL_DA_Q + h)),
            pl.BlockSpec((1, s, LANES), lambda bi, h, qi: (bi, 0, COL_DA_K + h)),
            pl.BlockSpec((1, s, LANES), lambda bi, h, qi: (bi, 0, COL_DA_V + h)),
            pl.BlockSpec((1, 3, T, T), lambda bi, h, qi: (h, 0, 0, 0)),
            pl.BlockSpec((1, LANES), lambda bi, h, qi: (0, 0)),
            pl.BlockSpec((1, LANES), lambda bi, h, qi: (0, 0)),
        ],
        out_specs=pl.BlockSpec((1, T, LANES), lambda bi, h, qi: (bi, qi, h)),
        out_shape=jax.ShapeDtypeStruct((b, s, MIX_WIDTH), BF16),
        compiler_params=_cparams(3, VMEM_LIMIT),
        name="diff_attn",
    )(proj3, proj3, proj3, bias_tbl, lam_row, g_row)


def _sb_kernel(q_ref, k_ref, v_ref, u_ref, o_ref, *, T):
    qi = pl.program_id(2)
    lane = lax.broadcasted_iota(jnp.int32, (T, LANES), 1)
    qs = q_ref[0] * jnp.asarray(SB_DIM ** -0.5, BF16)
    zero = jnp.zeros_like(qs)
    qheads = (jnp.where(lane < SB_DIM, qs, zero), jnp.where(lane >= SB_DIM, qs, zero))
    row = lax.broadcasted_iota(jnp.int32, (T, T), 0)
    col = lax.broadcasted_iota(jnp.int32, (T, T), 1)
    strict = col < row
    u = u_ref[...]

    def tile(j, carry, diag):
        r0 = pl.multiple_of(j * T, T)
        kt = k_ref[0, pl.ds(r0, T), :]
        vt = v_ref[0, pl.ds(r0, T), :]
        out = []
        for hh in range(2):
            c, acc = carry[hh]
            z = lax.dot_general(qheads[hh], kt, (((1,), (1,)), ((), ())),
                                preferred_element_type=F32)
            l1 = jnp.log(1.0 + jnp.exp(-jnp.abs(z)))
            ls = jnp.minimum(z, 0.0) - l1
            lk = ls - z
            if diag:
                lk = jnp.where(strict, lk, 0.0)
            later = jnp.dot(lk.astype(BF16), u, preferred_element_type=F32) + c
            w = jnp.exp(ls + later)
            if diag:
                w = jnp.where(strict, w, 0.0)
            acc = acc + jnp.dot(w.astype(BF16), vt, preferred_element_type=F32)
            c = c + jnp.sum(lk, axis=-1, keepdims=True)
            out.append((c, acc))
        return tuple(out)

    init = tuple((jnp.zeros((T, 1), F32), jnp.zeros((T, LANES), F32)) for _ in range(2))
    carry = tile(qi, init, True)
    carry = lax.fori_loop(0, qi, lambda jj, c: tile(qi - 1 - jj, c, False), carry)
    (_, a0), (_, a1) = carry
    o_ref[0] = jnp.where(lane < SB_DIM, a0, a1).astype(o_ref.dtype)


def _stick_breaking(proj3, u, T=ATT_TILE):
    b, s, _ = proj3.shape
    kern = functools.partial(_sb_kernel, T=T)
    return pl.pallas_call(
        kern,
        grid=(b, SB_HEADS // 2, s // T),
        in_specs=[
            pl.BlockSpec((1, T, LANES), lambda bi, h, qi: (bi, qi, COL_SB_Q + h)),
            pl.BlockSpec((1, s, LANES), lambda bi, h, qi: (bi, 0, COL_SB_K + h)),
            pl.BlockSpec((1, s, LANES), lambda bi, h, qi: (bi, 0, COL_SB_V + h)),
            pl.BlockSpec((T, T), lambda bi, h, qi: (0, 0)),
        ],
        out_specs=pl.BlockSpec((1, T, LANES), lambda bi, h, qi: (bi, qi, h)),
        out_shape=jax.ShapeDtypeStruct((b, s, MIX_WIDTH), BF16),
        compiler_params=_cparams(3, VMEM_LIMIT),
        name="stick_breaking",
    )(proj3, proj3, proj3, u)


def _hg_kernel(f_ref, i_ref, q_ref, g_ref, lb_ref, ng_ref, cs_ref, o_ref, *, S):
    L, C = HG_BLOCK, HG_SUB
    lb = lb_ref[...]
    ng = ng_ref[...]
    cs_lhs = cs_ref[...]
    ones = jnp.ones((LANES, LANES), BF16)
    sub = lax.broadcasted_iota(jnp.int32, (SUBLANES, LANES), 0)

    def block(bi, st):
        r0 = pl.multiple_of(bi * L, L)
        fp = f_ref[0, pl.ds(r0, L), :].astype(F32)
        qp = q_ref[0, pl.ds(r0, L), :].astype(F32)
        vb = i_ref[0, pl.ds(r0, L), :]
        gp = g_ref[0, pl.ds(r0, L), :].astype(F32)
        vv = vb.astype(F32)
        f = lb + (1.0 - lb) * _sigmoid(fp)
        lf = jnp.log(f)
        kk = 1.0 - f
        qq = qp * _sigmoid(qp)
        hi = lf.astype(BF16)
        lo = (lf - hi.astype(F32)).astype(BF16)
        cc = jnp.dot(cs_lhs, jnp.concatenate([hi, lo], axis=1), preferred_element_type=F32)
        cum = cc[:L, :LANES] + cc[:L, LANES:]
        tot = cc[L:, :LANES] + cc[L:, LANES:]
        qh = (qq * jnp.exp(cum)).astype(BF16)
        kd = (kk * jnp.exp(tot - cum)).astype(BF16)
        dec = jnp.exp(tot)
        outs = []
        for i in range(L // C):
            base = i * C
            sl = slice(base, base + C)
            inter = lax.dot_general(qh[sl], st.astype(BF16), (((1,), (1,)), ((), ())),
                                    preferred_element_type=F32)
            ut = lax.dot_general(vb[sl], kd[sl], (((0,), (0,)), ((), ())),
                                 preferred_element_type=F32)
            st = st * dec[base:base + 1, :] + ut
            pieces, meta = [], []
            for s in range(C):
                rs = base + s
                c_s = cum[rs:rs + 1, :]
                k_s = kk[rs:rs + 1, :]
                for tv in range(C // SUBLANES):
                    if tv < s // SUBLANES:
                        continue
                    rows = slice(base + tv * SUBLANES, base + (tv + 1) * SUBLANES)
                    dlt = cum[rows] - c_s
                    if tv == s // SUBLANES:
                        dlt = jnp.where(sub >= (s % SUBLANES), dlt, -jnp.inf)
                    pieces.append(qq[rows] * jnp.exp(dlt) * k_s)
                    meta.append((tv, rs))
            red = jnp.dot(jnp.concatenate(pieces, axis=0).astype(BF16), ones,
                          preferred_element_type=F32)
            intra = [jnp.zeros((SUBLANES, LANES), F32) for _ in range(C // SUBLANES)]
            for n, (tv, rs) in enumerate(meta):
                intra[tv] = intra[tv] + red[n * SUBLANES:(n + 1) * SUBLANES] * vv[rs:rs + 1, :]
            outs.append(inter + jnp.concatenate(intra, axis=0))
        o = jnp.concatenate(outs, axis=0)
        y = _rms(o, ng) * (gp * _sigmoid(gp))
        o_ref[0, pl.ds(r0, L), :] = y.astype(o_ref.dtype)
        return st

    lax.fori_loop(0, S // L, block, jnp.zeros((LANES, LANES), F32))


def _hgrn2(proj3, lb_row, ng_row, cs_lhs):
    b, s, _ = proj3.shape
    kern = functools.partial(_hg_kernel, S=s)
    col = lambda c: pl.BlockSpec((1, s, LANES), lambda bi, h: (bi, 0, c + h))
    return pl.pallas_call(
        kern,
        grid=(b, HG_HEADS),
        in_specs=[
            col(COL_HG_F), col(COL_HG_I), col(COL_HG_Q), col(COL_HG_G),
            pl.BlockSpec((1, LANES), lambda bi, h: (0, h)),
            pl.BlockSpec((1, LANES), lambda bi, h: (0, 0)),
            pl.BlockSpec((2 * HG_BLOCK, HG_BLOCK), lambda bi, h: (0, 0)),
        ],
        out_specs=pl.BlockSpec((1, s, LANES), lambda bi, h: (bi, 0, h)),
        out_shape=jax.ShapeDtypeStruct((b, s, MIX_WIDTH), BF16),
        compiler_params=_cparams(2, VMEM_LIMIT),
        name="hgrn2",
    )(proj3, proj3, proj3, proj3, lb_row, ng_row, cs_lhs)


def _merge_kernel(yda_ref, yhg_ref, ysb_ref, g0_ref, g1_ref, g2_ref, wup_ref, wout_ref, x_ref,
                  o_ref, m_ref):
    ys = (yda_ref, yhg_ref, ysb_ref)
    gs = (g0_ref, g1_ref, g2_ref)
    nc = 256
    for n in range(D_MODEL // nc):
        cs = slice(n * nc, (n + 1) * nc)
        acc = None
        for br in range(N_BRANCH):
            up = jnp.dot(ys[br][...], wup_ref[br, :, cs], preferred_element_type=F32)
            term = _sigmoid(gs[br][:, cs].astype(F32)) * up
            acc = term if acc is None else acc + term
        m_ref[:, cs] = acc.astype(BF16)
    o_ref[...] = x_ref[...] + jnp.dot(m_ref[...], wout_ref[...], preferred_element_type=F32)


def _merge(y_da, y_hg, y_sb, proj, w_up, w_out, x2d, tm=512):
    m, d = x2d.shape
    ysp = pl.BlockSpec((tm, MIX_WIDTH), lambda i: (i, 0))
    gate = lambda br: pl.BlockSpec((tm, d), lambda i: (i, COL_GATE * LANES // D_MODEL + br))
    return pl.pallas_call(
        _merge_kernel,
        grid=(m // tm,),
        in_specs=[
            ysp, ysp, ysp, gate(0), gate(1), gate(2),
            pl.BlockSpec((N_BRANCH, MIX_WIDTH, d), lambda i: (0, 0, 0)),
            pl.BlockSpec((d, d), lambda i: (0, 0)),
            pl.BlockSpec((tm, d), lambda i: (i, 0)),
        ],
        out_specs=pl.BlockSpec((tm, d), lambda i: (i, 0)),
        out_shape=jax.ShapeDtypeStruct((m, d), F32),
        scratch_shapes=[pltpu.VMEM((tm, d), BF16)],
        compiler_params=_cparams(1, VMEM_LIMIT),
        name="merge",
    )(y_da, y_hg, y_sb, proj, proj, proj, w_up, w_out, x2d)


def _ffn_kernel(x_ref, g_ref, w1_ref, w2_ref, o_ref, h_ref):
    x = x_ref[...]
    h_ref[...] = _rms(x, g_ref[...]).astype(BF16)
    o_ref[...] = x
    nc = FFN_CHUNK

    def body(c, _):
        c1 = pl.multiple_of(c * 2 * nc, 2 * nc)
        c2 = pl.multiple_of(c * nc, nc)
        ab = jnp.dot(h_ref[...], w1_ref[:, pl.ds(c1, 2 * nc)], preferred_element_type=F32)
        a = ab[:, :nc]
        g = (a * _sigmoid(a)) * ab[:, nc:]
        o_ref[...] += jnp.dot(g.astype(BF16), w2_ref[pl.ds(c2, nc), :], preferred_element_type=F32)
        return 0

    lax.fori_loop(0, FFN_HIDDEN // nc, body, 0)


def _ffn(x2d, g, w1, w2, tm=512):
    m, d = x2d.shape
    return pl.pallas_call(
        _ffn_kernel,
        grid=(m // tm,),
        in_specs=[
            pl.BlockSpec((tm, d), lambda i: (i, 0)),
            pl.BlockSpec((1, d), lambda i: (0, 0)),
            pl.BlockSpec((d, 2 * FFN_HIDDEN), lambda i: (0, 0)),
            pl.BlockSpec((FFN_HIDDEN, d), lambda i: (0, 0)),
        ],
        out_specs=pl.BlockSpec((tm, d), lambda i: (i, 0)),
        out_shape=jax.ShapeDtypeStruct((m, d), F32),
        scratch_shapes=[pltpu.VMEM((tm, d), BF16)],
        compiler_params=_cparams(1, VMEM_LIMIT),
        name="ffn",
    )(x2d, g, w1, w2)


def _norm_kernel(x_ref, g_ref, o_ref):
    o_ref[...] = _rms(x_ref[...], g_ref[...])


def _final_norm(x2d, g, tm=1024):
    m, d = x2d.shape
    return pl.pallas_call(
        _norm_kernel,
        grid=(m // tm,),
        in_specs=[pl.BlockSpec((tm, d), lambda i: (i, 0)), pl.BlockSpec((1, d), lambda i: (0, 0))],
        out_specs=pl.BlockSpec((tm, d), lambda i: (i, 0)),
        out_shape=jax.ShapeDtypeStruct((m, d), F32),
        compiler_params=_cparams(1),
        name="final_norm",
    )(x2d, g)


def _t5_bucket_ids(rel):
    half = REL_BUCKETS // 2
    max_exact = half // 2
    ret = (rel > 0).astype(jnp.int32) * half
    n = jnp.abs(rel)
    nf = jnp.maximum(n, 1).astype(F32)
    large = max_exact + (jnp.log(nf / max_exact) / math.log(REL_MAX_DIST / max_exact)
                         * (half - max_exact)).astype(jnp.int32)
    large = jnp.minimum(large, half - 1)
    return ret + jnp.where(n < max_exact, n, large)


def _bias_tiles(rel_bias, T):
    assert T >= REL_MAX_DIST
    rel0 = jnp.arange(T, dtype=jnp.int32)[None, :] - jnp.arange(T, dtype=jnp.int32)[:, None]
    rel = jnp.stack([rel0, rel0 - T, rel0 - 2 * T])
    tiles = rel_bias.astype(F32)[_t5_bucket_ids(rel)]
    return jnp.transpose(tiles, (3, 0, 1, 2))


def _hg_cumsum_lhs():
    r = np.arange(HG_BLOCK)[:, None]
    c = np.arange(HG_BLOCK)[None, :]
    same = (r // HG_SUB) == (c // HG_SUB)
    return jnp.asarray(np.concatenate([same & (c <= r), same], axis=0).astype(np.float32), BF16)


def _sb_later_matrix(T):
    j = np.arange(T)[:, None]
    s = np.arange(T)[None, :]
    return jnp.asarray((j > s).astype(np.float32), BF16)


def kernel(x, norm_mix_g, w_in, rel_bias, diff_lambda, diff_subln_g, hgrn_lb_logits, hgrn_norm_g,
           w_up, w_out, norm_ffn_g, w_ffn_in, w_ffn_out, final_norm_g):
    b, s, d = x.shape
    m = b * s
    depth = w_in.shape[0]
    T = min(ATT_TILE, s)

    lam_init = jnp.asarray([0.8 - 0.6 * math.exp(-0.3 * l) for l in range(depth)], F32)
    dl = diff_lambda.astype(F32)
    lam = (jnp.exp(jnp.sum(dl[:, 0] * dl[:, 1], axis=-1))
           - jnp.exp(jnp.sum(dl[:, 2] * dl[:, 3], axis=-1)) + lam_init)
    lam_rows = jnp.broadcast_to(lam[:, None, None], (depth, 1, LANES))
    da_g = (diff_subln_g.astype(F32) * (1.0 - lam_init)[:, None])[:, None, :]
    lb_all = jnp.cumsum(jax.nn.softmax(hgrn_lb_logits.astype(F32), axis=0), axis=0)
    lb = (lb_all - lb_all[0:1])[:, None, :]

    nchunk = FFN_HIDDEN // FFN_CHUNK
    w1 = w_ffn_in.astype(BF16)
    w1 = jnp.stack([w1[:, :, :FFN_HIDDEN].reshape(depth, d, nchunk, FFN_CHUNK),
                    w1[:, :, FFN_HIDDEN:].reshape(depth, d, nchunk, FFN_CHUNK)], axis=3)
    w1 = w1.reshape(depth, d, 2 * FFN_HIDDEN)

    bias_tbl = _bias_tiles(rel_bias, T)
    cs_lhs = _hg_cumsum_lhs()
    u = _sb_later_matrix(T)

    layers = dict(
        g_mix=norm_mix_g.astype(F32)[:, None, :],
        w_in=w_in.astype(BF16),
        lam=lam_rows,
        da_g=da_g,
        lb=lb,
        hg_g=hgrn_norm_g.astype(F32)[:, None, :],
        w_up=w_up.astype(BF16),
        w_out=w_out.astype(BF16),
        g_ffn=norm_ffn_g.astype(F32)[:, None, :],
        w1=w1,
        w2=w_ffn_out.astype(BF16),
    )

    def layer(x2d, p):
        proj = _inproj(x2d, p["g_mix"], p["w_in"])
        proj3 = proj.reshape(b, s, IN_WIDTH)
        y_da = _diff_attn(proj3, bias_tbl, p["lam"], p["da_g"], T)
        y_hg = _hgrn2(proj3, p["lb"], p["hg_g"], cs_lhs)
        y_sb = _stick_breaking(proj3, u, T)
        x2d = _merge(y_da.reshape(m, MIX_WIDTH), y_hg.reshape(m, MIX_WIDTH),
                     y_sb.reshape(m, MIX_WIDTH), proj, p["w_up"], p["w_out"], x2d)
        x2d = _ffn(x2d, p["g_ffn"], p["w1"], p["w2"])
        return x2d, None

    x2d, _ = lax.scan(layer, x.astype(F32).reshape(m, d), layers)
    out = _final_norm(x2d, final_norm_g.astype(F32)[None, :])
    return out.reshape(b, s, d).astype(x.dtype)
```

```python
import functools
import math

import numpy as np
import jax
import jax.numpy as jnp
from jax import lax
from jax.experimental import pallas as pl
from jax.experimental.pallas import tpu as pltpu

F32 = jnp.float32
BF16 = jnp.bfloat16

D_MODEL = 1024
MIX_WIDTH = D_MODEL // 2
N_BRANCH = 3
DA_QK_DIM = 64
DA_HEADS = 4
HG_HEADS = 4
SB_DIM = 64
SB_HEADS = 8
CHUNK = 64
REL_BUCKETS = 32
REL_MAX_DIST = 128
FFN_HIDDEN = 2816
IN_WIDTH = 10 * MIX_WIDTH + N_BRANCH * D_MODEL
NORM_EPS = 1e-6
LOG2E = math.log2(math.e)

LANES = 128
SUBLANES = 8
COL_DA_Q, COL_DA_K, COL_DA_V = 0, 4, 8
COL_HG_F, COL_HG_I, COL_HG_Q, COL_HG_G = 12, 16, 20, 24
COL_SB_Q, COL_SB_K, COL_SB_V = 28, 32, 36
COL_GATE = 40

ATT_TILE = 256
HG_BLOCK = 128
HG_SUB = 16
FFN_CHUNK = 256
VMEM_LIMIT = 56 * 1024 * 1024


def _cparams(n_axes, vmem=None):
    return pltpu.CompilerParams(
        dimension_semantics=("arbitrary",) * n_axes,
        vmem_limit_bytes=vmem,
    )


def _rms(x, g):
    ms = jnp.mean(x * x, axis=-1, keepdims=True)
    return x * lax.rsqrt(ms + NORM_EPS) * g


def _sigmoid(x):
    return 1.0 / (1.0 + jnp.exp(-x))


def _fold8(x, op):
    return op(x.reshape(x.shape[0] // SUBLANES, SUBLANES, x.shape[1]), axis=0)


def _resident(shape):
    return pl.BlockSpec(shape, lambda *_: (0,) * len(shape), pipeline_mode=pl.Buffered(1))


def _inproj_kernel(x_ref, g_ref, w_ref, o_ref, *, tn):
    h = _rms(x_ref[...], g_ref[...]).astype(BF16)
    for n in range(w_ref.shape[1] // tn):
        cs = slice(n * tn, (n + 1) * tn)
        o_ref[:, cs] = jnp.dot(h, w_ref[:, cs], preferred_element_type=F32).astype(o_ref.dtype)


def _inproj(x2d, g, w, tm=512, tn=1024):
    m, d = x2d.shape
    n = w.shape[1]
    tm = min(tm, m)
    return pl.pallas_call(
        functools.partial(_inproj_kernel, tn=tn),
        grid=(m // tm,),
        in_specs=[
            pl.BlockSpec((tm, d), lambda i: (i, 0)),
            _resident((1, d)),
            _resident((d, n)),
        ],
        out_specs=pl.BlockSpec((tm, n), lambda i: (i, 0)),
        out_shape=jax.ShapeDtypeStruct((m, n), BF16),
        compiler_params=_cparams(1, VMEM_LIMIT),
        name="inproj",
    )(x2d, g, w)


def _da_kernel(q_ref, k_ref, v_ref, bias_ref, lam_ref, g_ref, o_ref, s_ref, p_ref, vt_ref, *, T, S):
    row = lax.broadcasted_iota(jnp.int32, (T, T), 0)
    col = lax.broadcasted_iota(jnp.int32, (T, T), 1)
    chunk_mask = (row // CHUNK) <= (col // CHUNK)
    lane = lax.broadcasted_iota(jnp.int32, (T, LANES), 1)
    vt_ref[...] = v_ref[0].astype(F32).T.astype(BF16)
    lam = lam_ref[...]
    g = g_ref[...]
    far_bias = bias_ref[0, 2, 0:1, :]
    nq = S // T

    def scores(qi, c):
        q = q_ref[0, qi * T:(qi + 1) * T, :]
        qmap = jnp.where((lane >= DA_QK_DIM) == (c == 1), q, jnp.zeros_like(q))
        mx_near, mx_far = None, None
        for j in range(qi + 1):
            sl = slice(j * T, (j + 1) * T)
            sj = lax.dot_general(k_ref[0, sl, :], qmap, (((1,), (1,)), ((), ())),
                                 preferred_element_type=F32)
            if qi - j < 2:
                sj = sj + bias_ref[0, qi - j]
            if j == qi:
                sj = jnp.where(chunk_mask, sj, -jnp.inf)
            s_ref[c, sl, :] = sj
            t8 = _fold8(sj, jnp.max)
            if qi - j < 2:
                mx_near = t8 if mx_near is None else jnp.maximum(mx_near, t8)
            else:
                mx_far = t8 if mx_far is None else jnp.maximum(mx_far, t8)
        m = jnp.max(mx_near, axis=0, keepdims=True)
        if mx_far is None:
            return m, m
        m = jnp.maximum(m, jnp.max(mx_far, axis=0, keepdims=True) + far_bias)
        return m, m - far_bias

    def probs(qi, c, m, m_far):
        ls = None
        for j in range(qi + 1):
            sl = slice(j * T, (j + 1) * T)
            p = jnp.exp2(s_ref[c, sl, :] - (m if qi - j < 2 else m_far))
            t8 = _fold8(p, jnp.sum)
            ls = t8 if ls is None else ls + t8
            p_ref[c, sl, :] = p.astype(BF16)
        return jnp.sum(ls, axis=0, keepdims=True)

    def values(qi, c, l):
        nk = (qi + 1) * T
        acc = jnp.dot(vt_ref[:, :nk], p_ref[c, :nk, :], preferred_element_type=F32)
        return acc * (1.0 / l)

    items = [(qi, c) for qi in range(nq) for c in range(2)]
    stats = scores(*items[0])
    res = []
    for n, (qi, c) in enumerate(items):
        nxt = scores(*items[n + 1]) if n + 1 < len(items) else None
        l = probs(qi, c, *stats)
        res.append(values(qi, c, l))
        stats = nxt
        if c == 1:
            o = res[0] - lam * res[1]
            res = []
            ms = jnp.mean(o * o, axis=0, keepdims=True)
            y = o * lax.rsqrt(ms + NORM_EPS) * g
            o_ref[0, qi * T:(qi + 1) * T, :] = y.T.astype(o_ref.dtype)


def _diff_attn(proj3, bias_tbl, lam_row, g_col, T=ATT_TILE):
    b, s, _ = proj3.shape
    kern = functools.partial(_da_kernel, T=T, S=s)
    col = lambda c: pl.BlockSpec((1, s, LANES), lambda bi, h: (bi, 0, c + h))
    return pl.pallas_call(
        kern,
        grid=(b, DA_HEADS),
        in_specs=[
            col(COL_DA_Q), col(COL_DA_K), col(COL_DA_V),
            pl.BlockSpec((1, 3, T, T), lambda bi, h: (h, 0, 0, 0)),
            pl.BlockSpec((1, T), lambda bi, h: (0, 0)),
            pl.BlockSpec((LANES, T), lambda bi, h: (0, 0)),
        ],
        out_specs=pl.BlockSpec((1, s, LANES), lambda bi, h: (bi, 0, h)),
        out_shape=jax.ShapeDtypeStruct((b, s, MIX_WIDTH), BF16),
        scratch_shapes=[pltpu.VMEM((2, s, T), F32), pltpu.VMEM((2, s, T), BF16),
                        pltpu.VMEM((LANES, s), BF16)],
        compiler_params=_cparams(2, VMEM_LIMIT),
        name="diff_attn",
    )(proj3, proj3, proj3, bias_tbl, lam_row, g_col)


SB_SKIP_LOG2 = -160.0


def _sb_tile_logs(z, mask):
    neg_abs = lax.bitcast_convert_type(
        lax.bitcast_convert_type(z, jnp.uint32) | jnp.uint32(0x80000000), F32)
    l2 = jnp.log(1.0 + jnp.exp2(neg_abs)) * LOG2E
    ls = jnp.minimum(z, 0.0) - l2
    lk = ls - z
    if mask is not None:
        lk = jnp.where(mask, lk, 0.0)
    return ls, lk.astype(BF16), jnp.sum(_fold8(lk, jnp.sum), axis=0, keepdims=True)


def _sb_tile_weights(ls, later, carry, mask):
    w = jnp.exp2(ls + (later if carry is None else later + carry))
    if mask is not None:
        w = jnp.where(mask, w, 0.0)
    return w.astype(BF16)


def _sb_kernel(q_ref, k_ref, v_ref, ut_ref, o_ref, acc_ref, car_ref, vt_ref, *, T, S):
    row = lax.broadcasted_iota(jnp.int32, (T, T), 0)
    col = lax.broadcasted_iota(jnp.int32, (T, T), 1)
    strict = row < col
    lane = lax.broadcasted_iota(jnp.int32, (T, LANES), 1)
    head_rows = lax.broadcasted_iota(jnp.int32, (LANES, T), 0) < SB_DIM
    vt_ref[...] = v_ref[0].astype(F32).T.astype(BF16)
    ut = ut_ref[...]
    dn = (((1,), (1,)), ((), ()))
    nq = S // T

    def qhead(qi, hh):
        q = q_ref[0, qi * T:(qi + 1) * T, :]
        return jnp.where((lane >= SB_DIM) == (hh == 1), q, jnp.zeros_like(q))

    def emit(qi, accs):
        o = jnp.where(head_rows, accs[0], accs[1])
        o_ref[0, qi * T:(qi + 1) * T, :] = o.T.astype(o_ref.dtype)

    items = [(qi, hh, back) for qi in range(nq) for hh in range(2)
             for back in ((0, 1) if qi else (0,))]

    def stage_scores(item):
        qi, hh, back = item
        j = qi - back
        return lax.dot_general(k_ref[0, j * T:(j + 1) * T, :], qhead(qi, hh), dn,
                               preferred_element_type=F32)

    def stage_logs(item, z):
        ls, lkb, cs = _sb_tile_logs(z, strict if item[2] == 0 else None)
        return ls, jnp.dot(ut, lkb, preferred_element_type=F32), cs

    pend, done, far_carry = {}, {}, None

    def stage_weights(item, ls, later, cs):
        nonlocal far_carry
        qi, hh, back = item
        if back == 0:
            w = _sb_tile_weights(ls, later, None, strict)
            acc = jnp.dot(vt_ref[:, qi * T:(qi + 1) * T], w, preferred_element_type=F32)
            pend[(qi, hh)] = (acc, cs)
            if qi:
                return
        else:
            acc, carry = pend[(qi, hh)]
            w = _sb_tile_weights(ls, later, carry, None)
            acc = acc + jnp.dot(vt_ref[:, (qi - 1) * T:qi * T], w, preferred_element_type=F32)
            cs = carry + cs
        if qi >= 2:
            acc_ref[qi, hh] = acc
            car_ref[pl.ds(2 * qi + hh, 1), :] = cs
            far_carry = cs if far_carry is None else jnp.maximum(far_carry, cs)
        done[(qi, hh)] = acc
        if hh == 1:
            emit(qi, (done.pop((qi, 0)), done.pop((qi, 1))))

    zs, logs = {}, {}
    for step in range(len(items) + 2):
        if step < len(items):
            zs[step] = stage_scores(items[step])
        if 0 <= step - 1 < len(items):
            logs[step - 1] = stage_logs(items[step - 1], zs.pop(step - 1))
        if 0 <= step - 2 < len(items):
            stage_weights(items[step - 2], *logs.pop(step - 2))

    if nq <= 2:
        return

    @pl.when(jnp.max(far_carry) > SB_SKIP_LOG2)
    def _():
        for qi in range(2, nq):
            qheads = (qhead(qi, 0), qhead(qi, 1))

            def cond(st):
                j, c0, c1, _, _ = st
                return (j >= 0) & (jnp.maximum(jnp.max(c0), jnp.max(c1)) > SB_SKIP_LOG2)

            def body(st, qheads=qheads):
                j, c0, c1, a0, a1 = st
                r0 = pl.multiple_of(j * T, T)
                keys = k_ref[0, pl.ds(r0, T), :]
                vts = vt_ref[:, pl.ds(r0, T)]
                new = []
                for hh, (c, a) in enumerate(((c0, a0), (c1, a1))):
                    z = lax.dot_general(keys, qheads[hh], dn, preferred_element_type=F32)
                    ls, lkb, cs = _sb_tile_logs(z, None)
                    later = jnp.dot(ut, lkb, preferred_element_type=F32)
                    wb = _sb_tile_weights(ls, later, c, None)
                    new.append((c + cs, a + jnp.dot(vts, wb, preferred_element_type=F32)))
                return (j - 1, new[0][0], new[1][0], new[0][1], new[1][1])

            st = lax.while_loop(cond, body, (
                jnp.int32(qi - 2), car_ref[pl.ds(2 * qi, 1), :], car_ref[pl.ds(2 * qi + 1, 1), :],
                acc_ref[qi, 0], acc_ref[qi, 1]))
            emit(qi, (st[3], st[4]))


def _stick_breaking(proj3, ut, T=ATT_TILE):
    b, s, _ = proj3.shape
    kern = functools.partial(_sb_kernel, T=T, S=s)
    col = lambda c: pl.BlockSpec((1, s, LANES), lambda bi, h: (bi, 0, c + h))
    return pl.pallas_call(
        kern,
        grid=(b, SB_HEADS // 2),
        in_specs=[
            col(COL_SB_Q), col(COL_SB_K), col(COL_SB_V),
            pl.BlockSpec((T, T), lambda bi, h: (0, 0)),
        ],
        out_specs=pl.BlockSpec((1, s, LANES), lambda bi, h: (bi, 0, h)),
        out_shape=jax.ShapeDtypeStruct((b, s, MIX_WIDTH), BF16),
        scratch_shapes=[pltpu.VMEM((s // T, 2, LANES, T), F32), pltpu.VMEM((2 * (s // T), T), F32),
                        pltpu.VMEM((LANES, s), BF16)],
        compiler_params=_cparams(2, VMEM_LIMIT),
        name="stick_breaking",
    )(proj3, proj3, proj3, ut)


def _hg_stage_gates(fp, qp, lb, cs_lhs):
    L = HG_BLOCK
    f = lb + (1.0 - lb) * _sigmoid(fp)
    lf = jnp.log(f) * LOG2E
    kk = 1.0 - f
    qq = qp * _sigmoid(qp)
    hi = lf.astype(BF16)
    lo = (lf - hi.astype(F32)).astype(BF16)
    cc = jnp.dot(cs_lhs, jnp.concatenate([hi, lo], axis=1), preferred_element_type=F32)
    cum = cc[:L, :LANES] + cc[:L, LANES:]
    tot = cc[L:, :LANES] + cc[L:, LANES:]
    ck = cum - jnp.log(kk) * LOG2E
    return qq, cum, tot, ck


def _hg_stage_pairs(qq, cum, ck, ones2, sub):
    L, C = HG_BLOCK, HG_SUB
    pieces, meta = [], []
    for rs in range(L):
        s = rs % C
        base = rs - s
        ck_s = ck[rs:rs + 1, :]
        for tv in range(s // SUBLANES, C // SUBLANES):
            r8 = base + tv * SUBLANES
            dlt = cum[r8:r8 + SUBLANES] - ck_s
            if tv == s // SUBLANES:
                dlt = jnp.where(sub >= (s % SUBLANES), dlt, -jnp.inf)
            pieces.append(qq[r8:r8 + SUBLANES] * jnp.exp2(dlt))
            meta.append((r8, rs))
    half = len(pieces) // 2
    lhs = jnp.concatenate([jnp.concatenate(pieces[:half], axis=0),
                           jnp.concatenate(pieces[half:], axis=0)], axis=1).astype(BF16)
    return jnp.dot(lhs, ones2, preferred_element_type=F32), meta


def _hg_stage_intra(red, meta, vv):
    half = len(meta) // 2
    intra = {}
    for n, (r8, rs) in enumerate(meta):
        r = red[(n % half) * SUBLANES:(n % half + 1) * SUBLANES,
                (n // half) * LANES:(n // half + 1) * LANES]
        term = r * vv[rs:rs + 1, :]
        intra[r8] = term if r8 not in intra else intra[r8] + term
    return intra


def _hg_kernel(f_ref, i_ref, q_ref, g_ref, lb_ref, ng_ref, cs_ref, ones_ref, o_ref, *, S, NH):
    L, C = HG_BLOCK, HG_SUB
    ng = ng_ref[...]
    cs_lhs = cs_ref[...]
    ones2 = ones_ref[...]
    sub = lax.broadcasted_iota(jnp.int32, (SUBLANES, LANES), 0)
    heads = range(NH)

    def block(bi, sts):
        r0 = pl.multiple_of(bi * L, L)
        lanes = [slice(hd * LANES, (hd + 1) * LANES) for hd in heads]
        gates = [_hg_stage_gates(f_ref[0, pl.ds(r0, L), ls].astype(F32),
                                 q_ref[0, pl.ds(r0, L), ls].astype(F32), lb_ref[:, ls], cs_lhs)
                 for ls in lanes]
        vbs = [i_ref[0, pl.ds(r0, L), ls] for ls in lanes]
        qhs = [(qq * jnp.exp2(cum)).astype(BF16) for qq, cum, _, _ in gates]
        kds = [jnp.exp2(tot - ck).astype(BF16) for _, _, tot, ck in gates]
        reds = [_hg_stage_pairs(qq, cum, ck, ones2, sub) for qq, cum, _, ck in gates]
        uts = [[lax.dot_general(vbs[hd][i * C:(i + 1) * C], kds[hd][i * C:(i + 1) * C],
                                (((0,), (0,)), ((), ())), preferred_element_type=F32)
                for hd in heads] for i in range(L // C)]
        sts = list(sts)
        inters = [[] for _ in heads]
        for i in range(L // C):
            sl = slice(i * C, (i + 1) * C)
            for hd in heads:
                inters[hd].append(lax.dot_general(qhs[hd][sl], sts[hd].astype(BF16),
                                                  (((1,), (1,)), ((), ())),
                                                  preferred_element_type=F32))
                sts[hd] = sts[hd] * jnp.exp2(gates[hd][2][i * C:i * C + 1, :]) + uts[i][hd]
        ys = []
        for hd in heads:
            intra = _hg_stage_intra(*reds[hd], vbs[hd].astype(F32))
            o = jnp.concatenate(inters[hd], axis=0) + jnp.concatenate(
                [intra[r8] for r8 in range(0, L, SUBLANES)], axis=0)
            gp = g_ref[0, pl.ds(r0, L), lanes[hd]].astype(F32)
            ys.append(_rms(o, ng) * (gp * _sigmoid(gp)))
        o_ref[0, pl.ds(r0, L), :] = jnp.concatenate(ys, axis=1).astype(o_ref.dtype)
        return tuple(sts)

    lax.fori_loop(0, S // L, block, tuple(jnp.zeros((LANES, LANES), F32) for _ in heads))


def _hgrn2(proj3, lb_row, ng_row, cs_lhs, ones2, nh=HG_HEADS):
    b, s, _ = proj3.shape
    kern = functools.partial(_hg_kernel, S=s, NH=nh)
    w = nh * LANES
    col = lambda c: pl.BlockSpec((1, s, w), lambda bi, h: (bi, 0, c // nh + h))
    return pl.pallas_call(
        kern,
        grid=(b, HG_HEADS // nh),
        in_specs=[
            col(COL_HG_F), col(COL_HG_I), col(COL_HG_Q), col(COL_HG_G),
            pl.BlockSpec((1, w), lambda bi, h: (0, h)),
            pl.BlockSpec((1, LANES), lambda bi, h: (0, 0)),
            pl.BlockSpec((2 * HG_BLOCK, HG_BLOCK), lambda bi, h: (0, 0)),
            pl.BlockSpec((2 * LANES, 2 * LANES), lambda bi, h: (0, 0)),
        ],
        out_specs=pl.BlockSpec((1, s, w), lambda bi, h: (bi, 0, h)),
        out_shape=jax.ShapeDtypeStruct((b, s, MIX_WIDTH), BF16),
        compiler_params=_cparams(2, VMEM_LIMIT),
        name="hgrn2",
    )(proj3, proj3, proj3, proj3, lb_row, ng_row, cs_lhs, ones2)


def _merge_kernel(yda_ref, yhg_ref, ysb_ref, g0_ref, g1_ref, g2_ref, wup_ref, wout_ref, x_ref,
                  o_ref, m_ref):
    ys = (yda_ref, yhg_ref, ysb_ref)
    gs = (g0_ref, g1_ref, g2_ref)
    nc = 256
    for n in range(D_MODEL // nc):
        cs = slice(n * nc, (n + 1) * nc)
        acc = None
        for br in range(N_BRANCH):
            up = jnp.dot(ys[br][...], wup_ref[br, :, cs], preferred_element_type=F32)
            term = _sigmoid(gs[br][:, cs].astype(F32)) * up
            acc = term if acc is None else acc + term
        m_ref[:, cs] = acc.astype(BF16)
    o_ref[...] = x_ref[...] + jnp.dot(m_ref[...], wout_ref[...], preferred_element_type=F32)


def _merge(y_da, y_hg, y_sb, proj, w_up, w_out, x2d, tm=512):
    m, d = x2d.shape
    ysp = pl.BlockSpec((tm, MIX_WIDTH), lambda i: (i, 0))
    gate = lambda br: pl.BlockSpec((tm, d), lambda i: (i, COL_GATE * LANES // D_MODEL + br))
    return pl.pallas_call(
        _merge_kernel,
        grid=(m // tm,),
        in_specs=[
            ysp, ysp, ysp, gate(0), gate(1), gate(2),
            _resident((N_BRANCH, MIX_WIDTH, d)),
            _resident((d, d)),
            pl.BlockSpec((tm, d), lambda i: (i, 0)),
        ],
        out_specs=pl.BlockSpec((tm, d), lambda i: (i, 0)),
        out_shape=jax.ShapeDtypeStruct((m, d), F32),
        scratch_shapes=[pltpu.VMEM((tm, d), BF16)],
        compiler_params=_cparams(1, VMEM_LIMIT),
        name="merge",
    )(y_da, y_hg, y_sb, proj, proj, proj, w_up, w_out, x2d)


def _ffn_kernel(x_ref, g_ref, w1_ref, w2_ref, o_ref, h_ref, u_ref):
    x = x_ref[...]
    h_ref[...] = _rms(x, g_ref[...]).astype(BF16)
    nc = FFN_CHUNK
    nchunk = FFN_HIDDEN // nc

    def up(c):
        return jnp.dot(h_ref[...], w1_ref[:, c * 2 * nc:(c + 1) * 2 * nc],
                       preferred_element_type=F32)

    ab = up(0)
    for c in range(nchunk):
        nxt = up(c + 1) if c + 1 < nchunk else None
        a = ab[:, :nc]
        u_ref[:, c * nc:(c + 1) * nc] = ((a * _sigmoid(a)) * ab[:, nc:]).astype(BF16)
        ab = nxt
    o_ref[...] = x + jnp.dot(u_ref[...], w2_ref[...], preferred_element_type=F32)


def _ffn(x2d, g, w1, w2, tm=512):
    m, d = x2d.shape
    tm = min(tm, m)
    return pl.pallas_call(
        _ffn_kernel,
        grid=(m // tm,),
        in_specs=[
            pl.BlockSpec((tm, d), lambda i: (i, 0)),
            _resident((1, d)),
            _resident((d, 2 * FFN_HIDDEN)),
            _resident((FFN_HIDDEN, d)),
        ],
        out_specs=pl.BlockSpec((tm, d), lambda i: (i, 0)),
        out_shape=jax.ShapeDtypeStruct((m, d), F32),
        scratch_shapes=[pltpu.VMEM((tm, d), BF16), pltpu.VMEM((tm, FFN_HIDDEN), BF16)],
        compiler_params=_cparams(1, VMEM_LIMIT),
        name="ffn",
    )(x2d, g, w1, w2)


def _norm_kernel(x_ref, g_ref, o_ref):
    o_ref[...] = _rms(x_ref[...], g_ref[...])


def _final_norm(x2d, g, tm=1024):
    m, d = x2d.shape
    return pl.pallas_call(
        _norm_kernel,
        grid=(m // tm,),
        in_specs=[pl.BlockSpec((tm, d), lambda i: (i, 0)), pl.BlockSpec((1, d), lambda i: (0, 0))],
        out_specs=pl.BlockSpec((tm, d), lambda i: (i, 0)),
        out_shape=jax.ShapeDtypeStruct((m, d), F32),
        compiler_params=_cparams(1),
        name="final_norm",
    )(x2d, g)


def _t5_bucket_ids(rel):
    half = REL_BUCKETS // 2
    max_exact = half // 2
    ret = (rel > 0).astype(jnp.int32) * half
    n = jnp.abs(rel)
    nf = jnp.maximum(n, 1).astype(F32)
    large = max_exact + (jnp.log(nf / max_exact) / math.log(REL_MAX_DIST / max_exact)
                         * (half - max_exact)).astype(jnp.int32)
    large = jnp.minimum(large, half - 1)
    return ret + jnp.where(n < max_exact, n, large)


def _bias_tiles(rel_bias, T):
    assert T >= REL_MAX_DIST
    rel0 = jnp.arange(T, dtype=jnp.int32)[:, None] - jnp.arange(T, dtype=jnp.int32)[None, :]
    ids = _t5_bucket_ids(jnp.stack([rel0, rel0 - T, rel0 - 2 * T]))[None]
    table = rel_bias.astype(F32).T * LOG2E
    tiles = jnp.zeros((table.shape[0],) + ids.shape[1:], F32)
    for bkt in range(REL_BUCKETS):
        tiles = jnp.where(ids == bkt, table[:, bkt][:, None, None, None], tiles)
    return tiles


def _hg_cumsum_lhs():
    r = np.arange(HG_BLOCK)[:, None]
    c = np.arange(HG_BLOCK)[None, :]
    same = (r // HG_SUB) == (c // HG_SUB)
    return jnp.asarray(np.concatenate([same & (c <= r), same], axis=0).astype(np.float32), BF16)


def _sb_later_matrix(T):
    s = np.arange(T)[:, None]
    j = np.arange(T)[None, :]
    return jnp.asarray((j > s).astype(np.float32), BF16)


def _setup(b, s, norm_mix_g, w_in, rel_bias, diff_lambda, diff_subln_g, hgrn_lb_logits, hgrn_norm_g,
           w_up, w_out, norm_ffn_g, w_ffn_in, w_ffn_out):
    depth, d = w_in.shape[0], w_in.shape[1]
    T = min(ATT_TILE, s)
    lam_init = jnp.asarray([0.8 - 0.6 * math.exp(-0.3 * l) for l in range(depth)], F32)
    dl = diff_lambda.astype(F32)
    lam = (jnp.exp(jnp.sum(dl[:, 0] * dl[:, 1], axis=-1))
           - jnp.exp(jnp.sum(dl[:, 2] * dl[:, 3], axis=-1)) + lam_init)
    lam_rows = jnp.broadcast_to(lam[:, None, None], (depth, 1, T))
    da_g = diff_subln_g.astype(F32) * (1.0 - lam_init)[:, None]
    da_g = jnp.broadcast_to(da_g[:, :, None], (depth, LANES, T))
    lb_all = jnp.cumsum(jax.nn.softmax(hgrn_lb_logits.astype(F32), axis=0), axis=0)
    lb = (lb_all - lb_all[0:1])[:, None, :]

    col_scale = np.ones((IN_WIDTH,), np.float32)
    col_scale[COL_DA_Q * LANES:COL_DA_K * LANES] = LOG2E * DA_QK_DIM ** -0.5
    col_scale[COL_SB_Q * LANES:COL_SB_K * LANES] = LOG2E * SB_DIM ** -0.5
    w_in_s = (w_in.astype(F32) * jnp.asarray(col_scale)).astype(BF16)

    nchunk = FFN_HIDDEN // FFN_CHUNK
    w1 = w_ffn_in.astype(BF16)
    w1 = jnp.stack([w1[:, :, :FFN_HIDDEN].reshape(depth, d, nchunk, FFN_CHUNK),
                    w1[:, :, FFN_HIDDEN:].reshape(depth, d, nchunk, FFN_CHUNK)], axis=3)
    w1 = w1.reshape(depth, d, 2 * FFN_HIDDEN)

    layers = dict(
        g_mix=norm_mix_g.astype(F32)[:, None, :],
        w_in=w_in_s,
        lam=lam_rows,
        da_g=da_g,
        lb=lb,
        hg_g=hgrn_norm_g.astype(F32)[:, None, :],
        w_up=w_up.astype(BF16),
        w_out=w_out.astype(BF16),
        g_ffn=norm_ffn_g.astype(F32)[:, None, :],
        w1=w1,
        w2=w_ffn_out.astype(BF16),
    )
    ones2 = jnp.asarray(np.kron(np.eye(2, dtype=np.float32), np.ones((LANES, LANES), np.float32)), BF16)
    consts = dict(bias=_bias_tiles(rel_bias, T), cs=_hg_cumsum_lhs(), ones2=ones2,
                  u=_sb_later_matrix(T), T=T)
    return layers, consts


def kernel(x, norm_mix_g, w_in, rel_bias, diff_lambda, diff_subln_g, hgrn_lb_logits, hgrn_norm_g,
           w_up, w_out, norm_ffn_g, w_ffn_in, w_ffn_out, final_norm_g):
    b, s, d = x.shape
    m = b * s
    layers, consts = _setup(b, s, norm_mix_g, w_in, rel_bias, diff_lambda, diff_subln_g,
                            hgrn_lb_logits, hgrn_norm_g, w_up, w_out, norm_ffn_g, w_ffn_in, w_ffn_out)
    T = consts["T"]

    def layer(x2d, p):
        proj = _inproj(x2d, p["g_mix"], p["w_in"])
        proj3 = proj.reshape(b, s, IN_WIDTH)
        y_da = _diff_attn(proj3, consts["bias"], p["lam"], p["da_g"], T)
        y_hg = _hgrn2(proj3, p["lb"], p["hg_g"], consts["cs"], consts["ones2"])
        y_sb = _stick_breaking(proj3, consts["u"], T)
        x2d = _merge(y_da.reshape(m, MIX_WIDTH), y_hg.reshape(m, MIX_WIDTH),
                     y_sb.reshape(m, MIX_WIDTH), proj, p["w_up"], p["w_out"], x2d)
        x2d = _ffn(x2d, p["g_ffn"], p["w1"], p["w2"])
        return x2d, None

    x2d, _ = lax.scan(layer, x.astype(F32).reshape(m, d), layers)
    out = _final_norm(x2d, final_norm_g.astype(F32)[None, :])
    return out.reshape(b, s, d).astype(x.dtype)
```

```python
import functools
import math

import numpy as np
import jax
import jax.numpy as jnp
from jax import lax
from jax.experimental import pallas as pl
from jax.experimental.pallas import tpu as pltpu

F32 = jnp.float32
BF16 = jnp.bfloat16

D_MODEL = 1024
MIX_WIDTH = D_MODEL // 2
N_BRANCH = 3
DA_QK_DIM = 64
DA_HEADS = 4
HG_HEADS = 4
SB_DIM = 64
SB_HEADS = 8
CHUNK = 64
REL_BUCKETS = 32
REL_MAX_DIST = 128
FFN_HIDDEN = 2816
IN_WIDTH = 10 * MIX_WIDTH + N_BRANCH * D_MODEL
NORM_EPS = 1e-6
LOG2E = math.log2(math.e)

LANES = 128
SUBLANES = 8
COL_DA_Q, COL_DA_K, COL_DA_V = 0, 4, 8
COL_HG_F, COL_HG_I, COL_HG_Q, COL_HG_G = 12, 16, 20, 24
COL_SB_Q, COL_SB_K, COL_SB_V = 28, 32, 36
COL_GATE = 40

ATT_TILE = 256
HG_BLOCK = 128
HG_SUB = 16
FFN_CHUNK = 256
VMEM_LIMIT = 56 * 1024 * 1024


def _cparams(n_axes, vmem=None):
    return pltpu.CompilerParams(
        dimension_semantics=("arbitrary",) * n_axes,
        vmem_limit_bytes=vmem,
    )


def _rms(x, g):
    ms = jnp.mean(x * x, axis=-1, keepdims=True)
    return x * lax.rsqrt(ms + NORM_EPS) * g


def _sigmoid(x):
    return 1.0 / (1.0 + jnp.exp(-x))


def _fold8(x, op):
    return op(x.reshape(x.shape[0] // SUBLANES, SUBLANES, x.shape[1]), axis=0)


def _resident(shape):
    return pl.BlockSpec(shape, lambda *_: (0,) * len(shape), pipeline_mode=pl.Buffered(1))


def _inproj_kernel(x_ref, g_ref, w_ref, o_ref, *, tn):
    h = _rms(x_ref[...], g_ref[...]).astype(BF16)
    for n in range(w_ref.shape[1] // tn):
        cs = slice(n * tn, (n + 1) * tn)
        o_ref[:, cs] = jnp.dot(h, w_ref[:, cs], preferred_element_type=F32).astype(o_ref.dtype)


def _inproj(x2d, g, w, tm=512, tn=1024):
    m, d = x2d.shape
    n = w.shape[1]
    tm = min(tm, m)
    return pl.pallas_call(
        functools.partial(_inproj_kernel, tn=tn),
        grid=(m // tm,),
        in_specs=[
            pl.BlockSpec((tm, d), lambda i: (i, 0)),
            _resident((1, d)),
            _resident((d, n)),
        ],
        out_specs=pl.BlockSpec((tm, n), lambda i: (i, 0)),
        out_shape=jax.ShapeDtypeStruct((m, n), BF16),
        compiler_params=_cparams(1, VMEM_LIMIT),
        name="inproj",
    )(x2d, g, w)


def _da_kernel(q_ref, k_ref, v_ref, bias_ref, lam_ref, g_ref, o_ref, s_ref, p_ref, vt_ref, *, T, S):
    row = lax.broadcasted_iota(jnp.int32, (T, T), 0)
    col = lax.broadcasted_iota(jnp.int32, (T, T), 1)
    chunk_mask = (row // CHUNK) <= (col // CHUNK)
    lane = lax.broadcasted_iota(jnp.int32, (T, LANES), 1)
    vt_ref[...] = v_ref[0].astype(F32).T.astype(BF16)
    lam = lam_ref[...]
    g = g_ref[...]
    far_bias = bias_ref[0, 2, 0:1, :]
    nq = S // T

    def scores(qi, c):
        q = q_ref[0, qi * T:(qi + 1) * T, :]
        qmap = jnp.where((lane >= DA_QK_DIM) == (c == 1), q, jnp.zeros_like(q))
        mx_near, mx_far = None, None
        for j in range(qi + 1):
            sl = slice(j * T, (j + 1) * T)
            sj = lax.dot_general(k_ref[0, sl, :], qmap, (((1,), (1,)), ((), ())),
                                 preferred_element_type=F32)
            if qi - j < 2:
                sj = sj + bias_ref[0, qi - j]
            if j == qi:
                sj = jnp.where(chunk_mask, sj, -jnp.inf)
            s_ref[c, sl, :] = sj
            t8 = _fold8(sj, jnp.max)
            if qi - j < 2:
                mx_near = t8 if mx_near is None else jnp.maximum(mx_near, t8)
            else:
                mx_far = t8 if mx_far is None else jnp.maximum(mx_far, t8)
        m = jnp.max(mx_near, axis=0, keepdims=True)
        if mx_far is None:
            return m, m
        m = jnp.maximum(m, jnp.max(mx_far, axis=0, keepdims=True) + far_bias)
        return m, m - far_bias

    def probs(qi, c, m, m_far):
        ls = None
        for j in range(qi + 1):
            sl = slice(j * T, (j + 1) * T)
            p = jnp.exp2(s_ref[c, sl, :] - (m if qi - j < 2 else m_far))
            t8 = _fold8(p, jnp.sum)
            ls = t8 if ls is None else ls + t8
            p_ref[c, sl, :] = p.astype(BF16)
        return jnp.sum(ls, axis=0, keepdims=True)

    def values(qi, c, l):
        nk = (qi + 1) * T
        acc = jnp.dot(vt_ref[:, :nk], p_ref[c, :nk, :], preferred_element_type=F32)
        return acc * (1.0 / l)

    items = [(qi, c) for qi in range(nq) for c in range(2)]
    stats = scores(*items[0])
    res = []
    for n, (qi, c) in enumerate(items):
        nxt = scores(*items[n + 1]) if n + 1 < len(items) else None
        l = probs(qi, c, *stats)
        res.append(values(qi, c, l))
        stats = nxt
        if c == 1:
            o = res[0] - lam * res[1]
            res = []
            ms = jnp.mean(o * o, axis=0, keepdims=True)
            y = o * lax.rsqrt(ms + NORM_EPS) * g
            o_ref[0, qi * T:(qi + 1) * T, :] = y.T.astype(o_ref.dtype)


def _diff_attn(proj3, bias_tbl, lam_row, g_col, T=ATT_TILE):
    b, s, _ = proj3.shape
    kern = functools.partial(_da_kernel, T=T, S=s)
    col = lambda c: pl.BlockSpec((1, s, LANES), lambda bi, h: (bi, 0, c + h))
    return pl.pallas_call(
        kern,
        grid=(b, DA_HEADS),
        in_specs=[
            col(COL_DA_Q), col(COL_DA_K), col(COL_DA_V),
            pl.BlockSpec((1, 3, T, T), lambda bi, h: (h, 0, 0, 0)),
            pl.BlockSpec((1, T), lambda bi, h: (0, 0)),
            pl.BlockSpec((LANES, T), lambda bi, h: (0, 0)),
        ],
        out_specs=pl.BlockSpec((1, s, LANES), lambda bi, h: (bi, 0, h)),
        out_shape=jax.ShapeDtypeStruct((b, s, MIX_WIDTH), BF16),
        scratch_shapes=[pltpu.VMEM((2, s, T), F32), pltpu.VMEM((2, s, T), BF16),
                        pltpu.VMEM((LANES, s), BF16)],
        compiler_params=_cparams(2, VMEM_LIMIT),
        name="diff_attn",
    )(proj3, proj3, proj3, bias_tbl, lam_row, g_col)


SB_SKIP_LOG2 = -160.0


def _sb_tile_logs(z, mask):
    neg_abs = lax.bitcast_convert_type(
        lax.bitcast_convert_type(z, jnp.uint32) | jnp.uint32(0x80000000), F32)
    l2 = jnp.log(1.0 + jnp.exp2(neg_abs)) * LOG2E
    ls = jnp.minimum(z, 0.0) - l2
    lk = ls - z
    if mask is not None:
        lk = jnp.where(mask, lk, 0.0)
    return ls, lk.astype(BF16), jnp.sum(_fold8(lk, jnp.sum), axis=0, keepdims=True)


def _sb_tile_weights(ls, later, carry, mask):
    w = jnp.exp2(ls + (later if carry is None else later + carry))
    if mask is not None:
        w = jnp.where(mask, w, 0.0)
    return w.astype(BF16)


def _sb_kernel(q_ref, k_ref, v_ref, ut_ref, o_ref, acc_ref, car_ref, vt_ref, *, T, S):
    row = lax.broadcasted_iota(jnp.int32, (T, T), 0)
    col = lax.broadcasted_iota(jnp.int32, (T, T), 1)
    strict = row < col
    lane = lax.broadcasted_iota(jnp.int32, (T, LANES), 1)
    head_rows = lax.broadcasted_iota(jnp.int32, (LANES, T), 0) < SB_DIM
    vt_ref[...] = v_ref[0].astype(F32).T.astype(BF16)
    ut = ut_ref[...]
    dn = (((1,), (1,)), ((), ()))
    nq = S // T

    def qhead(qi, hh):
        q = q_ref[0, qi * T:(qi + 1) * T, :]
        return jnp.where((lane >= SB_DIM) == (hh == 1), q, jnp.zeros_like(q))

    def emit(qi, accs):
        o = jnp.where(head_rows, accs[0], accs[1])
        o_ref[0, qi * T:(qi + 1) * T, :] = o.T.astype(o_ref.dtype)

    items = [(qi, hh, back) for qi in range(nq) for hh in range(2)
             for back in ((0, 1) if qi else (0,))]

    def stage_scores(item):
        qi, hh, back = item
        j = qi - back
        return lax.dot_general(k_ref[0, j * T:(j + 1) * T, :], qhead(qi, hh), dn,
                               preferred_element_type=F32)

    def stage_logs(item, z):
        ls, lkb, cs = _sb_tile_logs(z, strict if item[2] == 0 else None)
        return ls, jnp.dot(ut, lkb, preferred_element_type=F32), cs

    pend, done, far_carry = {}, {}, None

    def stage_weights(item, ls, later, cs):
        nonlocal far_carry
        qi, hh, back = item
        if back == 0:
            w = _sb_tile_weights(ls, later, None, strict)
            acc = jnp.dot(vt_ref[:, qi * T:(qi + 1) * T], w, preferred_element_type=F32)
            pend[(qi, hh)] = (acc, cs)
            if qi:
                return
        else:
            acc, carry = pend[(qi, hh)]
            w = _sb_tile_weights(ls, later, carry, None)
            acc = acc + jnp.dot(vt_ref[:, (qi - 1) * T:qi * T], w, preferred_element_type=F32)
            cs = carry + cs
        if qi >= 2:
            acc_ref[qi, hh] = acc
            car_ref[pl.ds(2 * qi + hh, 1), :] = cs
            far_carry = cs if far_carry is None else jnp.maximum(far_carry, cs)
        done[(qi, hh)] = acc
        if hh == 1:
            emit(qi, (done.pop((qi, 0)), done.pop((qi, 1))))

    zs, logs = {}, {}
    for step in range(len(items) + 2):
        if step < len(items):
            zs[step] = stage_scores(items[step])
        if 0 <= step - 1 < len(items):
            logs[step - 1] = stage_logs(items[step - 1], zs.pop(step - 1))
        if 0 <= step - 2 < len(items):
            stage_weights(items[step - 2], *logs.pop(step - 2))

    if nq <= 2:
        return

    @pl.when(jnp.max(far_carry) > SB_SKIP_LOG2)
    def _():
        for qi in range(2, nq):
            qheads = (qhead(qi, 0), qhead(qi, 1))

            def cond(st):
                j, c0, c1, _, _ = st
                return (j >= 0) & (jnp.maximum(jnp.max(c0), jnp.max(c1)) > SB_SKIP_LOG2)

            def body(st, qheads=qheads):
                j, c0, c1, a0, a1 = st
                r0 = pl.multiple_of(j * T, T)
                keys = k_ref[0, pl.ds(r0, T), :]
                vts = vt_ref[:, pl.ds(r0, T)]
                new = []
                for hh, (c, a) in enumerate(((c0, a0), (c1, a1))):
                    z = lax.dot_general(keys, qheads[hh], dn, preferred_element_type=F32)
                    ls, lkb, cs = _sb_tile_logs(z, None)
                    later = jnp.dot(ut, lkb, preferred_element_type=F32)
                    wb = _sb_tile_weights(ls, later, c, None)
                    new.append((c + cs, a + jnp.dot(vts, wb, preferred_element_type=F32)))
                return (j - 1, new[0][0], new[1][0], new[0][1], new[1][1])

            st = lax.while_loop(cond, body, (
                jnp.int32(qi - 2), car_ref[pl.ds(2 * qi, 1), :], car_ref[pl.ds(2 * qi + 1, 1), :],
                acc_ref[qi, 0], acc_ref[qi, 1]))
            emit(qi, (st[3], st[4]))


def _stick_breaking(proj3, ut, T=ATT_TILE):
    b, s, _ = proj3.shape
    kern = functools.partial(_sb_kernel, T=T, S=s)
    col = lambda c: pl.BlockSpec((1, s, LANES), lambda bi, h: (bi, 0, c + h))
    return pl.pallas_call(
        kern,
        grid=(b, SB_HEADS // 2),
        in_specs=[
            col(COL_SB_Q), col(COL_SB_K), col(COL_SB_V),
            pl.BlockSpec((T, T), lambda bi, h: (0, 0)),
        ],
        out_specs=pl.BlockSpec((1, s, LANES), lambda bi, h: (bi, 0, h)),
        out_shape=jax.ShapeDtypeStruct((b, s, MIX_WIDTH), BF16),
        scratch_shapes=[pltpu.VMEM((s // T, 2, LANES, T), F32), pltpu.VMEM((2 * (s // T), T), F32),
                        pltpu.VMEM((LANES, s), BF16)],
        compiler_params=_cparams(2, VMEM_LIMIT),
        name="stick_breaking",
    )(proj3, proj3, proj3, ut)


def _hg_stage_gates(fp, qp, lb, cs_lhs):
    L = HG_BLOCK
    f = lb + (1.0 - lb) * _sigmoid(fp)
    lf = jnp.log(f) * LOG2E
    kk = 1.0 - f
    qq = qp * _sigmoid(qp)
    hi = lf.astype(BF16)
    lo = (lf - hi.astype(F32)).astype(BF16)
    cc = jnp.dot(cs_lhs, jnp.concatenate([hi, lo], axis=1), preferred_element_type=F32)
    cum = cc[:L, :LANES] + cc[:L, LANES:]
    tot = cc[L:, :LANES] + cc[L:, LANES:]
    ck = cum - jnp.log(kk) * LOG2E
    return qq, cum, tot, ck


def _hg_stage_pairs(qq, cum, ck, ones2, sub):
    L, C = HG_BLOCK, HG_SUB
    pieces, meta = [], []
    for rs in range(L):
        s = rs % C
        base = rs - s
        ck_s = ck[rs:rs + 1, :]
        for tv in range(s // SUBLANES, C // SUBLANES):
            r8 = base + tv * SUBLANES
            dlt = cum[r8:r8 + SUBLANES] - ck_s
            if tv == s // SUBLANES:
                dlt = jnp.where(sub >= (s % SUBLANES), dlt, -jnp.inf)
            pieces.append(qq[r8:r8 + SUBLANES] * jnp.exp2(dlt))
            meta.append((r8, rs))
    half = len(pieces) // 2
    lhs = jnp.concatenate([jnp.concatenate(pieces[:half], axis=0),
                           jnp.concatenate(pieces[half:], axis=0)], axis=1).astype(BF16)
    return jnp.dot(lhs, ones2, preferred_element_type=F32), meta


def _hg_stage_intra(red, meta, vv):
    half = len(meta) // 2
    intra = {}
    for n, (r8, rs) in enumerate(meta):
        r = red[(n % half) * SUBLANES:(n % half + 1) * SUBLANES,
                (n // half) * LANES:(n // half + 1) * LANES]
        term = r * vv[rs:rs + 1, :]
        intra[r8] = term if r8 not in intra else intra[r8] + term
    return intra


def _hg_kernel(f_ref, i_ref, q_ref, g_ref, lb_ref, ng_ref, cs_ref, ones_ref, o_ref, *, S, NH):
    L, C = HG_BLOCK, HG_SUB
    ng = ng_ref[...]
    cs_lhs = cs_ref[...]
    ones2 = ones_ref[...]
    sub = lax.broadcasted_iota(jnp.int32, (SUBLANES, LANES), 0)
    heads = range(NH)

    def block(bi, sts):
        r0 = pl.multiple_of(bi * L, L)
        lanes = [slice(hd * LANES, (hd + 1) * LANES) for hd in heads]
        gates = [_hg_stage_gates(f_ref[0, pl.ds(r0, L), ls].astype(F32),
                                 q_ref[0, pl.ds(r0, L), ls].astype(F32), lb_ref[:, ls], cs_lhs)
                 for ls in lanes]
        vbs = [i_ref[0, pl.ds(r0, L), ls] for ls in lanes]
        qhs = [(qq * jnp.exp2(cum)).astype(BF16) for qq, cum, _, _ in gates]
        kds = [jnp.exp2(tot - ck).astype(BF16) for _, _, tot, ck in gates]
        reds = [_hg_stage_pairs(qq, cum, ck, ones2, sub) for qq, cum, _, ck in gates]
        uts = [[lax.dot_general(vbs[hd][i * C:(i + 1) * C], kds[hd][i * C:(i + 1) * C],
                                (((0,), (0,)), ((), ())), preferred_element_type=F32)
                for hd in heads] for i in range(L // C)]
        sts = list(sts)
        inters = [[] for _ in heads]
        for i in range(L // C):
            sl = slice(i * C, (i + 1) * C)
            for hd in heads:
                inters[hd].append(lax.dot_general(qhs[hd][sl], sts[hd].astype(BF16),
                                                  (((1,), (1,)), ((), ())),
                                                  preferred_element_type=F32))
                sts[hd] = sts[hd] * jnp.exp2(gates[hd][2][i * C:i * C + 1, :]) + uts[i][hd]
        ys = []
        for hd in heads:
            intra = _hg_stage_intra(*reds[hd], vbs[hd].astype(F32))
            o = jnp.concatenate(inters[hd], axis=0) + jnp.concatenate(
                [intra[r8] for r8 in range(0, L, SUBLANES)], axis=0)
            gp = g_ref[0, pl.ds(r0, L), lanes[hd]].astype(F32)
            ys.append(_rms(o, ng) * (gp * _sigmoid(gp)))
        o_ref[0, pl.ds(r0, L), :] = jnp.concatenate(ys, axis=1).astype(o_ref.dtype)
        return tuple(sts)

    lax.fori_loop(0, S // L, block, tuple(jnp.zeros((LANES, LANES), F32) for _ in heads))


def _hgrn2(proj3, lb_row, ng_row, cs_lhs, ones2, nh=HG_HEADS):
    b, s, _ = proj3.shape
    kern = functools.partial(_hg_kernel, S=s, NH=nh)
    w = nh * LANES
    col = lambda c: pl.BlockSpec((1, s, w), lambda bi, h: (bi, 0, c // nh + h))
    return pl.pallas_call(
        kern,
        grid=(b, HG_HEADS // nh),
        in_specs=[
            col(COL_HG_F), col(COL_HG_I), col(COL_HG_Q), col(COL_HG_G),
            pl.BlockSpec((1, w), lambda bi, h: (0, h)),
            pl.BlockSpec((1, LANES), lambda bi, h: (0, 0)),
            pl.BlockSpec((2 * HG_BLOCK, HG_BLOCK), lambda bi, h: (0, 0)),
            pl.BlockSpec((2 * LANES, 2 * LANES), lambda bi, h: (0, 0)),
        ],
        out_specs=pl.BlockSpec((1, s, w), lambda bi, h: (bi, 0, h)),
        out_shape=jax.ShapeDtypeStruct((b, s, MIX_WIDTH), BF16),
        compiler_params=_cparams(2, VMEM_LIMIT),
        name="hgrn2",
    )(proj3, proj3, proj3, proj3, lb_row, ng_row, cs_lhs, ones2)


def _mix_ffn_kernel(*refs, final):
    (yda_ref, yhg_ref, ysb_ref, g0_ref, g1_ref, g2_ref, wup_ref, wout_ref, x_ref,
     gf_ref, w1_ref, w2_ref) = refs[:12]
    fg_ref = refs[12] if final else None
    o_ref, m_ref, h_ref, u_ref = refs[-4:]
    ys = (yda_ref, yhg_ref, ysb_ref)
    gs = (g0_ref, g1_ref, g2_ref)
    nc = 256

    def lift(n):
        return [jnp.dot(ys[br][...], wup_ref[br, :, n * nc:(n + 1) * nc], preferred_element_type=F32)
                for br in range(N_BRANCH)]

    ups = lift(0)
    for n in range(D_MODEL // nc):
        nxt = lift(n + 1) if (n + 1) * nc < D_MODEL else None
        acc = None
        for br in range(N_BRANCH):
            term = _sigmoid(gs[br][:, n * nc:(n + 1) * nc].astype(F32)) * ups[br]
            acc = term if acc is None else acc + term
        m_ref[:, n * nc:(n + 1) * nc] = acc.astype(BF16)
        ups = nxt
    x1 = x_ref[...] + jnp.dot(m_ref[...], wout_ref[...], preferred_element_type=F32)
    h_ref[...] = _rms(x1, gf_ref[...]).astype(BF16)
    o_ref[...] = x1

    nf = FFN_CHUNK

    def up(c):
        return jnp.dot(h_ref[...], w1_ref[:, c * 2 * nf:(c + 1) * 2 * nf],
                       preferred_element_type=F32)

    ab = up(0)
    for c in range(FFN_HIDDEN // nf):
        nxt = up(c + 1) if (c + 1) * nf < FFN_HIDDEN else None
        a = ab[:, :nf]
        u_ref[:, c * nf:(c + 1) * nf] = ((a * _sigmoid(a)) * ab[:, nf:]).astype(BF16)
        ab = nxt
    x2 = o_ref[...] + jnp.dot(u_ref[...], w2_ref[...], preferred_element_type=F32)
    o_ref[...] = _rms(x2, fg_ref[...]) if final else x2


def _mix_ffn(y_da, y_hg, y_sb, proj, w_up, w_out, x2d, g_ffn, w1, w2, final_g=None, tm=512):
    m, d = x2d.shape
    tm = min(tm, m)
    final = final_g is not None
    ysp = pl.BlockSpec((tm, MIX_WIDTH), lambda i: (i, 0))
    gate = lambda br: pl.BlockSpec((tm, d), lambda i: (i, COL_GATE * LANES // D_MODEL + br))
    in_specs = [
        ysp, ysp, ysp, gate(0), gate(1), gate(2),
        _resident((N_BRANCH, MIX_WIDTH, d)),
        _resident((d, d)),
        pl.BlockSpec((tm, d), lambda i: (i, 0)),
        _resident((1, d)),
        _resident((d, 2 * FFN_HIDDEN)),
        _resident((FFN_HIDDEN, d)),
    ]
    args = [y_da, y_hg, y_sb, proj, proj, proj, w_up, w_out, x2d, g_ffn, w1, w2]
    if final:
        in_specs.append(_resident((1, d)))
        args.append(final_g)
    return pl.pallas_call(
        functools.partial(_mix_ffn_kernel, final=final),
        grid=(m // tm,),
        in_specs=in_specs,
        out_specs=pl.BlockSpec((tm, d), lambda i: (i, 0)),
        out_shape=jax.ShapeDtypeStruct((m, d), F32),
        scratch_shapes=[pltpu.VMEM((tm, d), BF16), pltpu.VMEM((tm, d), BF16),
                        pltpu.VMEM((tm, FFN_HIDDEN), BF16)],
        compiler_params=_cparams(1, VMEM_LIMIT),
        name="mix_ffn_final" if final else "mix_ffn",
    )(*args)


def _t5_bucket_ids(rel):
    half = REL_BUCKETS // 2
    max_exact = half // 2
    ret = (rel > 0).astype(jnp.int32) * half
    n = jnp.abs(rel)
    nf = jnp.maximum(n, 1).astype(F32)
    large = max_exact + (jnp.log(nf / max_exact) / math.log(REL_MAX_DIST / max_exact)
                         * (half - max_exact)).astype(jnp.int32)
    large = jnp.minimum(large, half - 1)
    return ret + jnp.where(n < max_exact, n, large)


def _bias_tiles(rel_bias, T):
    assert T >= REL_MAX_DIST
    rel0 = jnp.arange(T, dtype=jnp.int32)[:, None] - jnp.arange(T, dtype=jnp.int32)[None, :]
    ids = _t5_bucket_ids(jnp.stack([rel0, rel0 - T, rel0 - 2 * T]))[None]
    table = rel_bias.astype(F32).T * LOG2E
    tiles = jnp.zeros((table.shape[0],) + ids.shape[1:], F32)
    for bkt in range(REL_BUCKETS):
        tiles = jnp.where(ids == bkt, table[:, bkt][:, None, None, None], tiles)
    return tiles


def _hg_cumsum_lhs():
    r = np.arange(HG_BLOCK)[:, None]
    c = np.arange(HG_BLOCK)[None, :]
    same = (r // HG_SUB) == (c // HG_SUB)
    return jnp.asarray(np.concatenate([same & (c <= r), same], axis=0).astype(np.float32), BF16)


def _sb_later_matrix(T):
    s = np.arange(T)[:, None]
    j = np.arange(T)[None, :]
    return jnp.asarray((j > s).astype(np.float32), BF16)


def _setup(b, s, norm_mix_g, w_in, rel_bias, diff_lambda, diff_subln_g, hgrn_lb_logits, hgrn_norm_g,
           w_up, w_out, norm_ffn_g, w_ffn_in, w_ffn_out):
    depth, d = w_in.shape[0], w_in.shape[1]
    T = min(ATT_TILE, s)
    lam_init = jnp.asarray([0.8 - 0.6 * math.exp(-0.3 * l) for l in range(depth)], F32)
    dl = diff_lambda.astype(F32)
    lam = (jnp.exp(jnp.sum(dl[:, 0] * dl[:, 1], axis=-1))
           - jnp.exp(jnp.sum(dl[:, 2] * dl[:, 3], axis=-1)) + lam_init)
    lam_rows = jnp.broadcast_to(lam[:, None, None], (depth, 1, T))
    da_g = diff_subln_g.astype(F32) * (1.0 - lam_init)[:, None]
    da_g = jnp.broadcast_to(da_g[:, :, None], (depth, LANES, T))
    lb_all = jnp.cumsum(jax.nn.softmax(hgrn_lb_logits.astype(F32), axis=0), axis=0)
    lb = (lb_all - lb_all[0:1])[:, None, :]

    col_scale = np.ones((IN_WIDTH,), np.float32)
    col_scale[COL_DA_Q * LANES:COL_DA_K * LANES] = LOG2E * DA_QK_DIM ** -0.5
    col_scale[COL_SB_Q * LANES:COL_SB_K * LANES] = LOG2E * SB_DIM ** -0.5
    w_in_s = (w_in.astype(F32) * jnp.asarray(col_scale)).astype(BF16)

    nchunk = FFN_HIDDEN // FFN_CHUNK
    w1 = w_ffn_in.astype(BF16)
    w1 = jnp.stack([w1[:, :, :FFN_HIDDEN].reshape(depth, d, nchunk, FFN_CHUNK),
                    w1[:, :, FFN_HIDDEN:].reshape(depth, d, nchunk, FFN_CHUNK)], axis=3)
    w1 = w1.reshape(depth, d, 2 * FFN_HIDDEN)

    layers = dict(
        g_mix=norm_mix_g.astype(F32)[:, None, :],
        w_in=w_in_s,
        lam=lam_rows,
        da_g=da_g,
        lb=lb,
        hg_g=hgrn_norm_g.astype(F32)[:, None, :],
        w_up=w_up.astype(BF16),
        w_out=w_out.astype(BF16),
        g_ffn=norm_ffn_g.astype(F32)[:, None, :],
        w1=w1,
        w2=w_ffn_out.astype(BF16),
    )
    ones2 = jnp.asarray(np.kron(np.eye(2, dtype=np.float32), np.ones((LANES, LANES), np.float32)), BF16)
    consts = dict(bias=_bias_tiles(rel_bias, T), cs=_hg_cumsum_lhs(), ones2=ones2,
                  u=_sb_later_matrix(T), T=T)
    return layers, consts


def kernel(x, norm_mix_g, w_in, rel_bias, diff_lambda, diff_subln_g, hgrn_lb_logits, hgrn_norm_g,
           w_up, w_out, norm_ffn_g, w_ffn_in, w_ffn_out, final_norm_g):
    b, s, d = x.shape
    m = b * s
    layers, consts = _setup(b, s, norm_mix_g, w_in, rel_bias, diff_lambda, diff_subln_g,
                            hgrn_lb_logits, hgrn_norm_g, w_up, w_out, norm_ffn_g, w_ffn_in, w_ffn_out)
    T = consts["T"]

    depth = w_in.shape[0]
    x2d = x.astype(F32).reshape(m, d)
    for l in range(depth):
        p = {k: v[l] for k, v in layers.items()}
        proj = _inproj(x2d, p["g_mix"], p["w_in"])
        proj3 = proj.reshape(b, s, IN_WIDTH)
        y_da = _diff_attn(proj3, consts["bias"], p["lam"], p["da_g"], T)
        y_hg = _hgrn2(proj3, p["lb"], p["hg_g"], consts["cs"], consts["ones2"])
        y_sb = _stick_breaking(proj3, consts["u"], T)
        final_g = final_norm_g.astype(F32)[None, :] if l == depth - 1 else None
        x2d = _mix_ffn(y_da.reshape(m, MIX_WIDTH), y_hg.reshape(m, MIX_WIDTH),
                       y_sb.reshape(m, MIX_WIDTH), proj, p["w_up"], p["w_out"], x2d,
                       p["g_ffn"], p["w1"], p["w2"], final_g)
    return x2d.reshape(b, s, d).astype(x.dtype)
```

```python
import functools
import math

import numpy as np
import jax
import jax.numpy as jnp
from jax import lax
from jax.experimental import pallas as pl
from jax.experimental.pallas import tpu as pltpu

F32 = jnp.float32
BF16 = jnp.bfloat16

D_MODEL = 1024
MIX_WIDTH = D_MODEL // 2
N_BRANCH = 3
DA_QK_DIM = 64
DA_HEADS = 4
HG_HEADS = 4
SB_DIM = 64
SB_HEADS = 8
CHUNK = 64
REL_BUCKETS = 32
REL_MAX_DIST = 128
FFN_HIDDEN = 2816
IN_WIDTH = 10 * MIX_WIDTH + N_BRANCH * D_MODEL
NORM_EPS = 1e-6
LOG2E = math.log2(math.e)

LANES = 128
SUBLANES = 8
COL_DA_Q, COL_DA_K, COL_DA_V = 0, 4, 8
COL_HG_F, COL_HG_I, COL_HG_Q, COL_HG_G = 12, 16, 20, 24
COL_SB_Q, COL_SB_K, COL_SB_V = 28, 32, 36
COL_GATE = 40

ATT_TILE = 256
HG_BLOCK = 128
HG_SUB = 16
HG_PAIR_SPLIT = 32
FFN_CHUNK = 256
VMEM_LIMIT = 56 * 1024 * 1024


def _cparams(n_axes, vmem=None):
    return pltpu.CompilerParams(
        dimension_semantics=("arbitrary",) * n_axes,
        vmem_limit_bytes=vmem,
    )


def _rms(x, g):
    ms = jnp.mean(x * x, axis=-1, keepdims=True)
    return x * lax.rsqrt(ms + NORM_EPS) * g


def _sigmoid(x):
    return 1.0 / (1.0 + jnp.exp(-x))


def _fold8(x, op):
    return op(x.reshape(x.shape[0] // SUBLANES, SUBLANES, x.shape[1]), axis=0)


def _resident(shape, layer=None):
    if layer is None:
        return pl.BlockSpec(shape, lambda *_: (0,) * len(shape), pipeline_mode=pl.Buffered(1))
    return pl.BlockSpec((None,) + tuple(shape), lambda *_: (layer,) + (0,) * len(shape),
                        pipeline_mode=pl.Buffered(1))


def _inproj_kernel(x_ref, g_ref, w_ref, o_ref, *, tn):
    h = _rms(x_ref[...], g_ref[...]).astype(BF16)
    for n in range(w_ref.shape[1] // tn):
        cs = slice(n * tn, (n + 1) * tn)
        o_ref[:, cs] = jnp.dot(h, w_ref[:, cs], preferred_element_type=F32).astype(o_ref.dtype)


def _inproj(x2d, g, w, layer, tm=512, tn=1024):
    m, d = x2d.shape
    n = w.shape[-1]
    tm = min(tm, m)
    return pl.pallas_call(
        functools.partial(_inproj_kernel, tn=tn),
        grid=(m // tm,),
        in_specs=[
            pl.BlockSpec((tm, d), lambda i: (i, 0)),
            _resident((1, d), layer),
            _resident((d, n), layer),
        ],
        out_specs=pl.BlockSpec((tm, n), lambda i: (i, 0)),
        out_shape=jax.ShapeDtypeStruct((m, n), BF16),
        compiler_params=_cparams(1, VMEM_LIMIT),
        name="inproj",
    )(x2d, g, w)


DA_STAGE_LAGS = (2, 3)


def _da_kernel(q_ref, k_ref, v_ref, bias_ref, lam_ref, g_ref, o_ref, s_ref, p_ref, vt_ref, *, T, S):
    row = lax.broadcasted_iota(jnp.int32, (T, T), 0)
    col = lax.broadcasted_iota(jnp.int32, (T, T), 1)
    chunk_mask = (row // CHUNK) <= (col // CHUNK)
    lane = lax.broadcasted_iota(jnp.int32, (T, LANES), 1)
    vt_ref[...] = v_ref[0].astype(F32).T.astype(BF16)
    lam = lam_ref[...]
    g = g_ref[...]
    far_bias = bias_ref[0, 2, 0:1, :]
    nq = S // T

    def scores(qi, c, slot):
        q = q_ref[0, qi * T:(qi + 1) * T, :]
        qmap = jnp.where((lane >= DA_QK_DIM) == (c == 1), q, jnp.zeros_like(q))
        mx_near, mx_far = None, None
        for j in range(qi + 1):
            sl = slice(j * T, (j + 1) * T)
            sj = lax.dot_general(k_ref[0, sl, :], qmap, (((1,), (1,)), ((), ())),
                                 preferred_element_type=F32)
            if qi - j < 2:
                sj = sj + bias_ref[0, qi - j]
            if j == qi:
                sj = jnp.where(chunk_mask, sj, -jnp.inf)
            s_ref[slot, sl, :] = sj
            t8 = _fold8(sj, jnp.max)
            if qi - j < 2:
                mx_near = t8 if mx_near is None else jnp.maximum(mx_near, t8)
            else:
                mx_far = t8 if mx_far is None else jnp.maximum(mx_far, t8)
        m = jnp.max(mx_near, axis=0, keepdims=True)
        if mx_far is None:
            return m, m
        m = jnp.maximum(m, jnp.max(mx_far, axis=0, keepdims=True) + far_bias)
        return m, m - far_bias

    def probs(qi, slot, m, m_far):
        ls = None
        for j in range(qi + 1):
            sl = slice(j * T, (j + 1) * T)
            p = jnp.exp2(s_ref[slot, sl, :] - (m if qi - j < 2 else m_far))
            t8 = _fold8(p, jnp.sum)
            ls = t8 if ls is None else ls + t8
            p_ref[slot, sl, :] = p.astype(BF16)
        return jnp.sum(ls, axis=0, keepdims=True)

    def values(qi, slot, l):
        nk = (qi + 1) * T
        acc = jnp.dot(vt_ref[:, :nk], p_ref[slot, :nk, :], preferred_element_type=F32)
        return acc * (1.0 / l)

    items = [(qi, c) for qi in range(nq) for c in range(2)]
    d1, d2 = DA_STAGE_LAGS
    nb = s_ref.shape[0]
    stats, sums, res = {}, {}, {}
    for step in range(len(items) + d2):
        if step < len(items):
            stats[step] = scores(*items[step], step % nb)
        n = step - d1
        if 0 <= n < len(items):
            sums[n] = probs(items[n][0], n % nb, *stats.pop(n))
        n = step - d2
        if 0 <= n < len(items):
            qi, c = items[n]
            res[c] = values(qi, n % nb, sums.pop(n))
            if c == 1:
                o = res[0] - lam * res[1]
                ms = jnp.mean(o * o, axis=0, keepdims=True)
                y = o * lax.rsqrt(ms + NORM_EPS) * g
                o_ref[0, qi * T:(qi + 1) * T, :] = y.T.astype(o_ref.dtype)


def _diff_attn(proj3, bias_tbl, lam_row, g_col, T=ATT_TILE):
    b, s, _ = proj3.shape
    kern = functools.partial(_da_kernel, T=T, S=s)
    col = lambda c: pl.BlockSpec((1, s, LANES), lambda bi, h: (bi, 0, c + h))
    return pl.pallas_call(
        kern,
        grid=(b, DA_HEADS),
        in_specs=[
            col(COL_DA_Q), col(COL_DA_K), col(COL_DA_V),
            pl.BlockSpec((1, 3, T, T), lambda bi, h: (h, 0, 0, 0)),
            pl.BlockSpec((1, T), lambda bi, h: (0, 0)),
            pl.BlockSpec((LANES, T), lambda bi, h: (0, 0)),
        ],
        out_specs=pl.BlockSpec((1, s, LANES), lambda bi, h: (bi, 0, h)),
        out_shape=jax.ShapeDtypeStruct((b, s, MIX_WIDTH), BF16),
        scratch_shapes=[pltpu.VMEM((DA_STAGE_LAGS[1] + 1, s, T), F32),
                        pltpu.VMEM((DA_STAGE_LAGS[1] + 1, s, T), BF16),
                        pltpu.VMEM((LANES, s), BF16)],
        compiler_params=_cparams(2, VMEM_LIMIT),
        name="diff_attn",
    )(proj3, proj3, proj3, bias_tbl, lam_row, g_col)


SB_SKIP_LOG2 = -160.0
SB_STAGE_LAGS = (2, 4)


def _sb_tile_logs(z, mask, ut):
    T = z.shape[0]
    neg_abs = lax.bitcast_convert_type(
        lax.bitcast_convert_type(z, jnp.uint32) | jnp.uint32(0x80000000), F32)
    l2 = jnp.log(1.0 + jnp.exp2(neg_abs)) * LOG2E
    ls = jnp.minimum(z, 0.0) - l2
    lk = ls - z
    if mask is not None:
        lk = jnp.where(mask, lk, 0.0)
    sums = jnp.dot(ut, lk.astype(BF16), preferred_element_type=F32)
    return ls, sums[:T], sums[T:T + 1]


def _sb_tile_weights(ls, later, carry, mask):
    w = jnp.exp2(ls + (later if carry is None else later + carry))
    if mask is not None:
        w = jnp.where(mask, w, 0.0)
    return w.astype(BF16)


def _sb_kernel(q_ref, k_ref, v_ref, ut_ref, o_ref, acc_ref, car_ref, vt_ref, *, T, S):
    row = lax.broadcasted_iota(jnp.int32, (T, T), 0)
    col = lax.broadcasted_iota(jnp.int32, (T, T), 1)
    strict = row < col
    lane = lax.broadcasted_iota(jnp.int32, (T, LANES), 1)
    head_rows = lax.broadcasted_iota(jnp.int32, (LANES, T), 0) < SB_DIM
    vt_ref[...] = v_ref[0].astype(F32).T.astype(BF16)
    ut = ut_ref[...]
    dn = (((1,), (1,)), ((), ()))
    nq = S // T

    def qhead(qi, hh):
        q = q_ref[0, qi * T:(qi + 1) * T, :]
        return jnp.where((lane >= SB_DIM) == (hh == 1), q, jnp.zeros_like(q))

    def emit(qi, accs):
        o = jnp.where(head_rows, accs[0], accs[1])
        o_ref[0, qi * T:(qi + 1) * T, :] = o.T.astype(o_ref.dtype)

    items = [(qi, hh, back) for qi in range(nq) for hh in range(2)
             for back in ((0, 1) if qi else (0,))]

    def stage_scores(item):
        qi, hh, back = item
        j = qi - back
        return lax.dot_general(k_ref[0, j * T:(j + 1) * T, :], qhead(qi, hh), dn,
                               preferred_element_type=F32)

    def stage_logs(item, z):
        return _sb_tile_logs(z, strict if item[2] == 0 else None, ut)

    pend, done, far_carry = {}, {}, None

    def stage_weights(item, ls, later, cs):
        nonlocal far_carry
        qi, hh, back = item
        if back == 0:
            w = _sb_tile_weights(ls, later, None, strict)
            acc = jnp.dot(vt_ref[:, qi * T:(qi + 1) * T], w, preferred_element_type=F32)
            pend[(qi, hh)] = (acc, cs)
            if qi:
                return
        else:
            acc, carry = pend[(qi, hh)]
            w = _sb_tile_weights(ls, later, carry, None)
            acc = acc + jnp.dot(vt_ref[:, (qi - 1) * T:qi * T], w, preferred_element_type=F32)
            cs = carry + cs
        if qi >= 2:
            acc_ref[qi, hh] = acc
            car_ref[pl.ds(2 * qi + hh, 1), :] = cs
            far_carry = cs if far_carry is None else jnp.maximum(far_carry, cs)
        done[(qi, hh)] = acc
        if hh == 1:
            emit(qi, (done.pop((qi, 0)), done.pop((qi, 1))))

    zs, logs = {}, {}
    d1, d2 = SB_STAGE_LAGS
    for step in range(len(items) + d2):
        if step < len(items):
            zs[step] = stage_scores(items[step])
        if 0 <= step - d1 < len(items):
            logs[step - d1] = stage_logs(items[step - d1], zs.pop(step - d1))
        if 0 <= step - d2 < len(items):
            stage_weights(items[step - d2], *logs.pop(step - d2))

    if nq <= 2:
        return

    @pl.when(jnp.max(far_carry) > SB_SKIP_LOG2)
    def _():
        for qi in range(2, nq):
            qheads = (qhead(qi, 0), qhead(qi, 1))

            def cond(st):
                j, c0, c1, _, _ = st
                return (j >= 0) & (jnp.maximum(jnp.max(c0), jnp.max(c1)) > SB_SKIP_LOG2)

            def body(st, qheads=qheads):
                j, c0, c1, a0, a1 = st
                r0 = pl.multiple_of(j * T, T)
                keys = k_ref[0, pl.ds(r0, T), :]
                vts = vt_ref[:, pl.ds(r0, T)]
                new = []
                for hh, (c, a) in enumerate(((c0, a0), (c1, a1))):
                    z = lax.dot_general(keys, qheads[hh], dn, preferred_element_type=F32)
                    ls, later, cs = _sb_tile_logs(z, None, ut)
                    wb = _sb_tile_weights(ls, later, c, None)
                    new.append((c + cs, a + jnp.dot(vts, wb, preferred_element_type=F32)))
                return (j - 1, new[0][0], new[1][0], new[0][1], new[1][1])

            st = lax.while_loop(cond, body, (
                jnp.int32(qi - 2), car_ref[pl.ds(2 * qi, 1), :], car_ref[pl.ds(2 * qi + 1, 1), :],
                acc_ref[qi, 0], acc_ref[qi, 1]))
            emit(qi, (st[3], st[4]))


def _stick_breaking(proj3, ut, T=ATT_TILE):
    b, s, _ = proj3.shape
    kern = functools.partial(_sb_kernel, T=T, S=s)
    col = lambda c: pl.BlockSpec((1, s, LANES), lambda bi, h: (bi, 0, c + h))
    return pl.pallas_call(
        kern,
        grid=(b, SB_HEADS // 2),
        in_specs=[
            col(COL_SB_Q), col(COL_SB_K), col(COL_SB_V),
            pl.BlockSpec((T + SUBLANES, T), lambda bi, h: (0, 0)),
        ],
        out_specs=pl.BlockSpec((1, s, LANES), lambda bi, h: (bi, 0, h)),
        out_shape=jax.ShapeDtypeStruct((b, s, MIX_WIDTH), BF16),
        scratch_shapes=[pltpu.VMEM((s // T, 2, LANES, T), F32), pltpu.VMEM((2 * (s // T), T), F32),
                        pltpu.VMEM((LANES, s), BF16)],
        compiler_params=_cparams(2, VMEM_LIMIT),
        name="stick_breaking",
    )(proj3, proj3, proj3, ut)


def _hg_stage_gates(fp, qp, lb, cs_lhs):
    L = HG_BLOCK
    f = lb + (1.0 - lb) * _sigmoid(fp)
    lf = jnp.log(f) * LOG2E
    kk = 1.0 - f
    qq = qp * _sigmoid(qp)
    hi = lf.astype(BF16)
    lo = (lf - hi.astype(F32)).astype(BF16)
    cc = jnp.dot(cs_lhs, jnp.concatenate([hi, lo], axis=1), preferred_element_type=F32)
    cum = cc[:L, :LANES] + cc[:L, LANES:]
    tot = cc[L:, :LANES] + cc[L:, LANES:]
    ck = cum - jnp.log(kk) * LOG2E
    return qq, cum, tot, ck


def _hg_stage_pairs(qq, cum, ck, ones2, sub, rows):
    C = HG_SUB
    pieces, meta = [], []
    for rs in rows:
        s = rs % C
        base = rs - s
        ck_s = ck[rs:rs + 1, :]
        for tv in range(s // SUBLANES, C // SUBLANES):
            r8 = base + tv * SUBLANES
            dlt = cum[r8:r8 + SUBLANES] - ck_s
            if tv == s // SUBLANES:
                dlt = jnp.where(sub >= (s % SUBLANES), dlt, -jnp.inf)
            pieces.append(qq[r8:r8 + SUBLANES] * jnp.exp2(dlt))
            meta.append((r8, rs))
    half = len(pieces) // 2
    lhs = jnp.concatenate([jnp.concatenate(pieces[:half], axis=0),
                           jnp.concatenate(pieces[half:], axis=0)], axis=1).astype(BF16)
    return jnp.dot(lhs, ones2, preferred_element_type=F32), meta


def _hg_stage_intra(red, meta, vv, intra):
    half = len(meta) // 2
    for n, (r8, rs) in enumerate(meta):
        r = red[(n % half) * SUBLANES:(n % half + 1) * SUBLANES,
                (n // half) * LANES:(n // half + 1) * LANES]
        term = r * vv[rs:rs + 1, :]
        intra[r8] = term if r8 not in intra else intra[r8] + term
    return intra


def _hg_kernel(f_ref, i_ref, q_ref, g_ref, lb_ref, ng_ref, cs_ref, ones_ref, o_ref, row_ref, *,
               S, NH, SPLIT):
    L, C = HG_BLOCK, HG_SUB
    ng = ng_ref[...]
    cs_lhs = cs_ref[...]
    ones2 = ones_ref[...]
    sub = lax.broadcasted_iota(jnp.int32, (SUBLANES, LANES), 0)
    heads = range(NH)
    lanes = [slice(hd * LANES, (hd + 1) * LANES) for hd in heads]

    def block(bi, sts):
        r0 = pl.multiple_of(bi * L, L)
        gates = [_hg_stage_gates(f_ref[0, pl.ds(r0, L), ls].astype(F32),
                                 q_ref[0, pl.ds(r0, L), ls].astype(F32), lb_ref[:, ls], cs_lhs)
                 for ls in lanes]
        vbs = [i_ref[0, pl.ds(r0, L), ls] for ls in lanes]
        qhs = [(qq * jnp.exp2(cum)).astype(BF16) for qq, cum, _, _ in gates]
        kds = [jnp.exp2(tot - ck).astype(BF16) for _, _, tot, ck in gates]
        for hd in heads:
            row_ref[hd, 0] = gates[hd][3]
            row_ref[hd, 1] = vbs[hd].astype(F32)
        reds_a = [_hg_stage_pairs(gates[hd][0], gates[hd][1], row_ref.at[hd, 0], ones2, sub,
                                  range(0, SPLIT)) for hd in heads]
        uts = [[lax.dot_general(vbs[hd][i * C:(i + 1) * C], kds[hd][i * C:(i + 1) * C],
                                (((0,), (0,)), ((), ())), preferred_element_type=F32)
                for hd in heads] for i in range(L // C)]
        sts = list(sts)
        inters = [[] for _ in heads]
        for i in range(L // C):
            sl = slice(i * C, (i + 1) * C)
            for hd in heads:
                inters[hd].append(lax.dot_general(qhs[hd][sl], sts[hd].astype(BF16),
                                                  (((1,), (1,)), ((), ())),
                                                  preferred_element_type=F32))
                sts[hd] = sts[hd] * jnp.exp2(gates[hd][2][i * C:i * C + 1, :]) + uts[i][hd]
        reds_b = [_hg_stage_pairs(gates[hd][0], gates[hd][1], row_ref.at[hd, 0], ones2, sub,
                                  range(SPLIT, L)) for hd in heads]
        ys = []
        for hd in heads:
            vv = row_ref.at[hd, 1]
            intra = _hg_stage_intra(*reds_a[hd], vv, {})
            intra = _hg_stage_intra(*reds_b[hd], vv, intra)
            o = jnp.concatenate(inters[hd], axis=0) + jnp.concatenate(
                [intra[r8] for r8 in range(0, L, SUBLANES)], axis=0)
            gp = g_ref[0, pl.ds(r0, L), lanes[hd]].astype(F32)
            ys.append(_rms(o, ng) * (gp * _sigmoid(gp)))
        o_ref[0, pl.ds(r0, L), :] = jnp.concatenate(ys, axis=1).astype(o_ref.dtype)
        return tuple(sts)

    lax.fori_loop(0, S // L, block, tuple(jnp.zeros((LANES, LANES), F32) for _ in heads))


def _hgrn2(proj3, lb_row, ng_row, cs_lhs, ones2, nh=HG_HEADS):
    b, s, _ = proj3.shape
    kern = functools.partial(_hg_kernel, S=s, NH=nh, SPLIT=HG_PAIR_SPLIT)
    w = nh * LANES
    col = lambda c: pl.BlockSpec((1, s, w), lambda bi, h: (bi, 0, c // nh + h))
    return pl.pallas_call(
        kern,
        grid=(b, HG_HEADS // nh),
        in_specs=[
            col(COL_HG_F), col(COL_HG_I), col(COL_HG_Q), col(COL_HG_G),
            pl.BlockSpec((1, w), lambda bi, h: (0, h)),
            pl.BlockSpec((1, LANES), lambda bi, h: (0, 0)),
            pl.BlockSpec((2 * HG_BLOCK, HG_BLOCK), lambda bi, h: (0, 0)),
            pl.BlockSpec((2 * LANES, 2 * LANES), lambda bi, h: (0, 0)),
        ],
        out_specs=pl.BlockSpec((1, s, w), lambda bi, h: (bi, 0, h)),
        out_shape=jax.ShapeDtypeStruct((b, s, MIX_WIDTH), BF16),
        scratch_shapes=[pltpu.VMEM((nh, 2, HG_BLOCK, LANES), F32)],
        compiler_params=_cparams(2, VMEM_LIMIT),
        name="hgrn2",
    )(proj3, proj3, proj3, proj3, lb_row, ng_row, cs_lhs, ones2)


def _mix_ffn_kernel(*refs, final):
    (yda_ref, yhg_ref, ysb_ref, g0_ref, g1_ref, g2_ref, wup_ref, wout_ref, x_ref,
     gf_ref, w1_ref, w2_ref) = refs[:12]
    fg_ref = refs[12] if final else None
    o_ref, m_ref, h_ref, u_ref = refs[-4:]
    ys = (yda_ref, yhg_ref, ysb_ref)
    gs = (g0_ref, g1_ref, g2_ref)
    nc = 256

    def lift(n):
        return [jnp.dot(ys[br][...], wup_ref[br, :, n * nc:(n + 1) * nc], preferred_element_type=F32)
                for br in range(N_BRANCH)]

    ups = lift(0)
    for n in range(D_MODEL // nc):
        nxt = lift(n + 1) if (n + 1) * nc < D_MODEL else None
        acc = None
        for br in range(N_BRANCH):
            term = _sigmoid(gs[br][:, n * nc:(n + 1) * nc].astype(F32)) * ups[br]
            acc = term if acc is None else acc + term
        m_ref[:, n * nc:(n + 1) * nc] = acc.astype(BF16)
        ups = nxt
    x1 = x_ref[...] + jnp.dot(m_ref[...], wout_ref[...], preferred_element_type=F32)
    h_ref[...] = _rms(x1, gf_ref[...]).astype(BF16)
    o_ref[...] = x1

    nf = FFN_CHUNK

    def up(c):
        h = h_ref[...]
        return (jnp.dot(h, w1_ref[:, c * nf:(c + 1) * nf], preferred_element_type=F32),
                jnp.dot(h, w1_ref[:, FFN_HIDDEN + c * nf:FFN_HIDDEN + (c + 1) * nf],
                        preferred_element_type=F32))

    ab = up(0)
    for c in range(FFN_HIDDEN // nf):
        nxt = up(c + 1) if (c + 1) * nf < FFN_HIDDEN else None
        a, bv = ab
        u_ref[:, c * nf:(c + 1) * nf] = ((a * _sigmoid(a)) * bv).astype(BF16)
        ab = nxt
    x2 = o_ref[...] + jnp.dot(u_ref[...], w2_ref[...], preferred_element_type=F32)
    o_ref[...] = _rms(x2, fg_ref[...]) if final else x2


def _mix_ffn(y_da, y_hg, y_sb, proj, w_up, w_out, x2d, g_ffn, w1, w2, layer, final_g=None, tm=512):
    m, d = x2d.shape
    tm = min(tm, m)
    final = final_g is not None
    ysp = pl.BlockSpec((tm, MIX_WIDTH), lambda i: (i, 0))
    gate = lambda br: pl.BlockSpec((tm, d), lambda i: (i, COL_GATE * LANES // D_MODEL + br))
    in_specs = [
        ysp, ysp, ysp, gate(0), gate(1), gate(2),
        _resident((N_BRANCH, MIX_WIDTH, d), layer),
        _resident((d, d), layer),
        pl.BlockSpec((tm, d), lambda i: (i, 0)),
        _resident((1, d), layer),
        _resident((d, 2 * FFN_HIDDEN), layer),
        _resident((FFN_HIDDEN, d), layer),
    ]
    args = [y_da, y_hg, y_sb, proj, proj, proj, w_up, w_out, x2d, g_ffn, w1, w2]
    if final:
        in_specs.append(_resident((1, d)))
        args.append(final_g)
    return pl.pallas_call(
        functools.partial(_mix_ffn_kernel, final=final),
        grid=(m // tm,),
        in_specs=in_specs,
        out_specs=pl.BlockSpec((tm, d), lambda i: (i, 0)),
        out_shape=jax.ShapeDtypeStruct((m, d), F32),
        scratch_shapes=[pltpu.VMEM((tm, d), BF16), pltpu.VMEM((tm, d), BF16),
                        pltpu.VMEM((tm, FFN_HIDDEN), BF16)],
        compiler_params=_cparams(1, VMEM_LIMIT),
        name="mix_ffn_final" if final else "mix_ffn",
    )(*args)


def _t5_bucket_ids(rel):
    half = REL_BUCKETS // 2
    max_exact = half // 2
    ret = (rel > 0).astype(jnp.int32) * half
    n = jnp.abs(rel)
    nf = jnp.maximum(n, 1).astype(F32)
    large = max_exact + (jnp.log(nf / max_exact) / math.log(REL_MAX_DIST / max_exact)
                         * (half - max_exact)).astype(jnp.int32)
    large = jnp.minimum(large, half - 1)
    return ret + jnp.where(n < max_exact, n, large)


def _bias_tiles(rel_bias, T):
    assert T >= REL_MAX_DIST
    rel0 = jnp.arange(T, dtype=jnp.int32)[:, None] - jnp.arange(T, dtype=jnp.int32)[None, :]
    ids = _t5_bucket_ids(jnp.stack([rel0, rel0 - T, rel0 - 2 * T]))[None]
    table = rel_bias.astype(F32).T * LOG2E
    tiles = jnp.zeros((table.shape[0],) + ids.shape[1:], F32)
    for bkt in range(REL_BUCKETS):
        tiles = jnp.where(ids == bkt, table[:, bkt][:, None, None, None], tiles)
    return tiles


def _hg_cumsum_lhs():
    r = np.arange(HG_BLOCK)[:, None]
    c = np.arange(HG_BLOCK)[None, :]
    same = (r // HG_SUB) == (c // HG_SUB)
    return jnp.asarray(np.concatenate([same & (c <= r), same], axis=0).astype(np.float32), BF16)


def _sb_later_matrix(T):
    s = np.arange(T)[:, None]
    j = np.arange(T)[None, :]
    later = (j > s).astype(np.float32)
    return jnp.asarray(np.concatenate([later, np.ones((SUBLANES, T), np.float32)], axis=0), BF16)


def _setup(b, s, norm_mix_g, w_in, rel_bias, diff_lambda, diff_subln_g, hgrn_lb_logits, hgrn_norm_g,
           w_up, w_out, norm_ffn_g, w_ffn_in, w_ffn_out):
    depth, d = w_in.shape[0], w_in.shape[1]
    T = min(ATT_TILE, s)
    lam_init = jnp.asarray([0.8 - 0.6 * math.exp(-0.3 * l) for l in range(depth)], F32)
    dl = diff_lambda.astype(F32)
    lam = (jnp.exp(jnp.sum(dl[:, 0] * dl[:, 1], axis=-1))
           - jnp.exp(jnp.sum(dl[:, 2] * dl[:, 3], axis=-1)) + lam_init)
    lam_rows = jnp.broadcast_to(lam[:, None, None], (depth, 1, T))
    da_g = diff_subln_g.astype(F32) * (1.0 - lam_init)[:, None]
    da_g = jnp.broadcast_to(da_g[:, :, None], (depth, LANES, T))
    lb_all = jnp.cumsum(jax.nn.softmax(hgrn_lb_logits.astype(F32), axis=0), axis=0)
    lb = (lb_all - lb_all[0:1])[:, None, :]

    col_scale = np.ones((IN_WIDTH,), np.float32)
    col_scale[COL_DA_Q * LANES:COL_DA_K * LANES] = LOG2E * DA_QK_DIM ** -0.5
    col_scale[COL_SB_Q * LANES:COL_SB_K * LANES] = LOG2E * SB_DIM ** -0.5
    w_in_s = (w_in.astype(F32) * jnp.asarray(col_scale)).astype(BF16)

    layers = dict(
        g_mix=norm_mix_g.astype(F32)[:, None, :],
        w_in=w_in_s,
        lam=lam_rows,
        da_g=da_g,
        lb=lb,
        hg_g=hgrn_norm_g.astype(F32)[:, None, :],
        w_up=w_up.astype(BF16),
        w_out=w_out.astype(BF16),
        g_ffn=norm_ffn_g.astype(F32)[:, None, :],
        w1=w_ffn_in.astype(BF16),
        w2=w_ffn_out.astype(BF16),
    )
    ones2 = jnp.asarray(np.kron(np.eye(2, dtype=np.float32), np.ones((LANES, LANES), np.float32)), BF16)
    consts = dict(bias=_bias_tiles(rel_bias, T), cs=_hg_cumsum_lhs(), ones2=ones2,
                  u=_sb_later_matrix(T), T=T)
    return layers, consts


def kernel(x, norm_mix_g, w_in, rel_bias, diff_lambda, diff_subln_g, hgrn_lb_logits, hgrn_norm_g,
           w_up, w_out, norm_ffn_g, w_ffn_in, w_ffn_out, final_norm_g):
    b, s, d = x.shape
    m = b * s
    layers, consts = _setup(b, s, norm_mix_g, w_in, rel_bias, diff_lambda, diff_subln_g,
                            hgrn_lb_logits, hgrn_norm_g, w_up, w_out, norm_ffn_g, w_ffn_in, w_ffn_out)
    T = consts["T"]

    depth = w_in.shape[0]
    x2d = x.astype(F32).reshape(m, d)
    p = layers
    for l in range(depth):
        proj = _inproj(x2d, p["g_mix"], p["w_in"], l)
        proj3 = proj.reshape(b, s, IN_WIDTH)
        y_da = _diff_attn(proj3, consts["bias"], p["lam"][l], p["da_g"][l], T)
        y_hg = _hgrn2(proj3, p["lb"][l], p["hg_g"][l], consts["cs"], consts["ones2"])
        y_sb = _stick_breaking(proj3, consts["u"], T)
        final_g = final_norm_g.astype(F32)[None, :] if l == depth - 1 else None
        x2d = _mix_ffn(y_da.reshape(m, MIX_WIDTH), y_hg.reshape(m, MIX_WIDTH),
                       y_sb.reshape(m, MIX_WIDTH), proj, p["w_up"], p["w_out"], x2d,
                       p["g_ffn"], p["w1"], p["w2"], l, final_g)
    return x2d.reshape(b, s, d).astype(x.dtype)
```

```python
import functools
import math

import numpy as np
import jax
import jax.numpy as jnp
from jax import lax
from jax.experimental import pallas as pl
from jax.experimental.pallas import tpu as pltpu

F32 = jnp.float32
BF16 = jnp.bfloat16

D_MODEL = 1024
MIX_WIDTH = D_MODEL // 2
N_BRANCH = 3
DA_QK_DIM = 64
DA_HEADS = 4
HG_HEADS = 4
SB_DIM = 64
SB_HEADS = 8
CHUNK = 64
REL_BUCKETS = 32
REL_MAX_DIST = 128
FFN_HIDDEN = 2816
IN_WIDTH = 10 * MIX_WIDTH + N_BRANCH * D_MODEL
NORM_EPS = 1e-6
LOG2E = math.log2(math.e)

LANES = 128
SUBLANES = 8
COL_DA_Q, COL_DA_K, COL_DA_V = 0, 4, 8
COL_HG_F, COL_HG_I, COL_HG_Q, COL_HG_G = 12, 16, 20, 24
COL_SB_Q, COL_SB_K, COL_SB_V = 28, 32, 36
COL_GATE = 40

ATT_TILE = 256
HG_BLOCK = 128
HG_SUB = 16
HG_STEP = 32
HG_PAIR_SPLIT = 16
FFN_CHUNK = 256
VMEM_LIMIT = 56 * 1024 * 1024


def _cparams(n_axes, vmem=None):
    return pltpu.CompilerParams(
        dimension_semantics=("arbitrary",) * n_axes,
        vmem_limit_bytes=vmem,
    )


def _rms(x, g):
    ms = jnp.mean(x * x, axis=-1, keepdims=True)
    return x * lax.rsqrt(ms + NORM_EPS) * g


def _sigmoid(x):
    return 1.0 / (1.0 + jnp.exp(-x))


def _fold8(x, op):
    return op(x.reshape(x.shape[0] // SUBLANES, SUBLANES, x.shape[1]), axis=0)


def _resident(shape, layer=None):
    if layer is None:
        return pl.BlockSpec(shape, lambda *_: (0,) * len(shape), pipeline_mode=pl.Buffered(1))
    return pl.BlockSpec((None,) + tuple(shape), lambda *_: (layer,) + (0,) * len(shape),
                        pipeline_mode=pl.Buffered(1))


def _inproj_kernel(x_ref, g_ref, w_ref, o_ref, *, tn):
    h = _rms(x_ref[...], g_ref[...]).astype(BF16)
    for n in range(w_ref.shape[1] // tn):
        cs = slice(n * tn, (n + 1) * tn)
        o_ref[:, cs] = jnp.dot(h, w_ref[:, cs], preferred_element_type=F32).astype(o_ref.dtype)


def _inproj(x2d, g, w, layer, tm=512, tn=1024):
    m, d = x2d.shape
    n = w.shape[-1]
    tm = min(tm, m)
    return pl.pallas_call(
        functools.partial(_inproj_kernel, tn=tn),
        grid=(m // tm,),
        in_specs=[
            pl.BlockSpec((tm, d), lambda i: (i, 0)),
            _resident((1, d), layer),
            _resident((d, n), layer),
        ],
        out_specs=pl.BlockSpec((tm, n), lambda i: (i, 0)),
        out_shape=jax.ShapeDtypeStruct((m, n), BF16),
        compiler_params=_cparams(1, VMEM_LIMIT),
        name="inproj",
    )(x2d, g, w)


DA_STAGE_LAGS = (2, 3)
DA_KEY_ROWS = 256


def _da_kernel(q_ref, k_ref, v_ref, bias_ref, lam_ref, g_ref, o_ref, s_ref, p_ref, vt_ref, *, T, S):
    row = lax.broadcasted_iota(jnp.int32, (T, T), 0)
    col = lax.broadcasted_iota(jnp.int32, (T, T), 1)
    chunk_mask = (row // CHUNK) <= (col // CHUNK)
    lane = lax.broadcasted_iota(jnp.int32, (T, LANES), 1)
    vt_ref[...] = v_ref[0].astype(F32).T.astype(BF16)
    lam = lam_ref[...]
    g = g_ref[...]
    far_bias = bias_ref[0, 2, 0:1, :]
    nq = S // T
    R = min(DA_KEY_ROWS, T)

    def scores(qi, c, slot):
        q = q_ref[0, qi * T:(qi + 1) * T, :]
        qmap = jnp.where((lane >= DA_QK_DIM) == (c == 1), q, jnp.zeros_like(q))
        mx_near, mx_far = None, None
        for r0 in range(0, (qi + 1) * T, R):
            j, off = divmod(r0, T)
            sl = slice(r0, r0 + R)
            sj = lax.dot_general(k_ref[0, sl, :], qmap, (((1,), (1,)), ((), ())),
                                 preferred_element_type=F32)
            if qi - j < 2:
                sj = sj + bias_ref[0, qi - j, off:off + R, :]
            if j == qi:
                sj = jnp.where(chunk_mask[off:off + R], sj, -jnp.inf)
            s_ref[slot, sl, :] = sj
            t8 = _fold8(sj, jnp.max)
            if qi - j < 2:
                mx_near = t8 if mx_near is None else jnp.maximum(mx_near, t8)
            else:
                mx_far = t8 if mx_far is None else jnp.maximum(mx_far, t8)
        m = jnp.max(mx_near, axis=0, keepdims=True)
        if mx_far is None:
            return m, m
        m = jnp.maximum(m, jnp.max(mx_far, axis=0, keepdims=True) + far_bias)
        return m, m - far_bias

    def probs(qi, slot, m, m_far):
        ls = None
        for r0 in range(0, (qi + 1) * T, R):
            j = r0 // T
            sl = slice(r0, r0 + R)
            p = jnp.exp2(s_ref[slot, sl, :] - (m if qi - j < 2 else m_far))
            t8 = _fold8(p, jnp.sum)
            ls = t8 if ls is None else ls + t8
            p_ref[slot, sl, :] = p.astype(BF16)
        return jnp.sum(ls, axis=0, keepdims=True)

    def values(qi, slot, l):
        nk = (qi + 1) * T
        acc = jnp.dot(vt_ref[:, :nk], p_ref[slot, :nk, :], preferred_element_type=F32)
        return acc * (1.0 / l)

    items = [(qi, c) for qi in range(nq) for c in range(2)]
    d1, d2 = DA_STAGE_LAGS
    nb = s_ref.shape[0]
    stats, sums, res = {}, {}, {}
    for step in range(len(items) + d2):
        if step < len(items):
            stats[step] = scores(*items[step], step % nb)
        n = step - d1
        if 0 <= n < len(items):
            sums[n] = probs(items[n][0], n % nb, *stats.pop(n))
        n = step - d2
        if 0 <= n < len(items):
            qi, c = items[n]
            res[c] = values(qi, n % nb, sums.pop(n))
            if c == 1:
                o = res[0] - lam * res[1]
                ms = jnp.mean(o * o, axis=0, keepdims=True)
                y = o * lax.rsqrt(ms + NORM_EPS) * g
                o_ref[0, qi * T:(qi + 1) * T, :] = y.T.astype(o_ref.dtype)


def _diff_attn(proj3, bias_tbl, lam_row, g_col, T=ATT_TILE):
    b, s, _ = proj3.shape
    kern = functools.partial(_da_kernel, T=T, S=s)
    col = lambda c: pl.BlockSpec((1, s, LANES), lambda h, bi: (bi, 0, c + h))
    return pl.pallas_call(
        kern,
        grid=(DA_HEADS, b),
        in_specs=[
            col(COL_DA_Q), col(COL_DA_K), col(COL_DA_V),
            pl.BlockSpec((1, 3, T, T), lambda h, bi: (h, 0, 0, 0)),
            pl.BlockSpec((1, T), lambda h, bi: (0, 0)),
            pl.BlockSpec((LANES, T), lambda h, bi: (0, 0)),
        ],
        out_specs=pl.BlockSpec((1, s, LANES), lambda h, bi: (bi, 0, h)),
        out_shape=jax.ShapeDtypeStruct((b, s, MIX_WIDTH), BF16),
        scratch_shapes=[pltpu.VMEM((DA_STAGE_LAGS[1] + 1, s, T), F32),
                        pltpu.VMEM((DA_STAGE_LAGS[1] + 1, s, T), BF16),
                        pltpu.VMEM((LANES, s), BF16)],
        compiler_params=_cparams(2, VMEM_LIMIT),
        name="diff_attn",
    )(proj3, proj3, proj3, bias_tbl, lam_row, g_col)


SB_SKIP_LOG2 = -160.0
SB_STAGE_LAGS = (2, 4)


def _sb_logsig(z, mask):
    neg_abs = lax.bitcast_convert_type(
        lax.bitcast_convert_type(z, jnp.uint32) | jnp.uint32(0x80000000), F32)
    l2 = jnp.log(1.0 + jnp.exp2(neg_abs)) * LOG2E
    ls = jnp.minimum(z, 0.0) - l2
    lk = ls - z
    return ls, (lk if mask is None else jnp.where(mask, lk, 0.0))


def _sb_tile_logs(z, mask, ut):
    T = z.shape[0]
    ls, lk = _sb_logsig(z, mask)
    sums = jnp.dot(ut, lk.astype(BF16), preferred_element_type=F32)
    return ls, sums[:T], sums[T:T + 1]


def _sb_tile_weights(ls, later, carry, mask):
    w = jnp.exp2(ls + (later if carry is None else later + carry))
    if mask is not None:
        w = jnp.where(mask, w, 0.0)
    return w.astype(BF16)


def _sb_diag_logs(z, strict, ut):
    T = z.shape[0]
    H = T // 2
    ls_top, lk_top = _sb_logsig(z[:H], strict[:H])
    ls_br, lk_br = _sb_logsig(z[H:, H:], strict[H:, H:])
    lk = jnp.concatenate(
        [lk_top, jnp.concatenate([jnp.zeros((H, H), F32), lk_br], axis=1)], axis=0)
    sums = jnp.dot(ut, lk.astype(BF16), preferred_element_type=F32)
    return (ls_top, ls_br), sums[:T], sums[T:T + 1]


def _sb_diag_weights(ls_parts, later, strict):
    ls_top, ls_br = ls_parts
    H = ls_br.shape[0]
    w_top = jnp.where(strict[:H], jnp.exp2(ls_top + later[:H]), 0.0)
    w_br = jnp.where(strict[H:, H:], jnp.exp2(ls_br + later[H:, H:]), 0.0)
    w = jnp.concatenate(
        [w_top, jnp.concatenate([jnp.zeros((H, H), F32), w_br], axis=1)], axis=0)
    return w.astype(BF16)


def _sb_kernel(q_ref, k_ref, v_ref, ut_ref, o_ref, acc_ref, car_ref, vt_ref, *, T, S):
    row = lax.broadcasted_iota(jnp.int32, (T, T), 0)
    col = lax.broadcasted_iota(jnp.int32, (T, T), 1)
    strict = row < col
    lane = lax.broadcasted_iota(jnp.int32, (T, LANES), 1)
    head_rows = lax.broadcasted_iota(jnp.int32, (LANES, T), 0) < SB_DIM
    vt_ref[...] = v_ref[0].astype(F32).T.astype(BF16)
    ut = ut_ref[...]
    dn = (((1,), (1,)), ((), ()))
    nq = S // T

    def qhead(qi, hh):
        q = q_ref[0, qi * T:(qi + 1) * T, :]
        return jnp.where((lane >= SB_DIM) == (hh == 1), q, jnp.zeros_like(q))

    def emit(qi, accs):
        o = jnp.where(head_rows, accs[0], accs[1])
        o_ref[0, qi * T:(qi + 1) * T, :] = o.T.astype(o_ref.dtype)

    items = [(qi, hh, back) for qi in range(nq) for hh in range(2)
             for back in ((0, 1) if qi else (0,))]

    def stage_scores(item):
        qi, hh, back = item
        j = qi - back
        return lax.dot_general(k_ref[0, j * T:(j + 1) * T, :], qhead(qi, hh), dn,
                               preferred_element_type=F32)

    def stage_logs(item, z):
        return _sb_diag_logs(z, strict, ut) if item[2] == 0 else _sb_tile_logs(z, None, ut)

    pend, done, far_carry = {}, {}, None

    def stage_weights(item, ls, later, cs):
        nonlocal far_carry
        qi, hh, back = item
        if back == 0:
            w = _sb_diag_weights(ls, later, strict)
            acc = jnp.dot(vt_ref[:, qi * T:(qi + 1) * T], w, preferred_element_type=F32)
            pend[(qi, hh)] = (acc, cs)
            if qi:
                return
        else:
            acc, carry = pend[(qi, hh)]
            w = _sb_tile_weights(ls, later, carry, None)
            acc = acc + jnp.dot(vt_ref[:, (qi - 1) * T:qi * T], w, preferred_element_type=F32)
            cs = carry + cs
        if qi >= 2:
            acc_ref[qi, hh] = acc
            car_ref[pl.ds(2 * qi + hh, 1), :] = cs
            far_carry = cs if far_carry is None else jnp.maximum(far_carry, cs)
        done[(qi, hh)] = acc
        if hh == 1:
            emit(qi, (done.pop((qi, 0)), done.pop((qi, 1))))

    zs, logs = {}, {}
    d1, d2 = SB_STAGE_LAGS
    for step in range(len(items) + d2):
        if step < len(items):
            zs[step] = stage_scores(items[step])
        if 0 <= step - d1 < len(items):
            logs[step - d1] = stage_logs(items[step - d1], zs.pop(step - d1))
        if 0 <= step - d2 < len(items):
            stage_weights(items[step - d2], *logs.pop(step - d2))

    if nq <= 2:
        return

    @pl.when(jnp.max(far_carry) > SB_SKIP_LOG2)
    def _():
        for qi in range(2, nq):
            qheads = (qhead(qi, 0), qhead(qi, 1))

            def cond(st):
                j, c0, c1, _, _ = st
                return (j >= 0) & (jnp.maximum(jnp.max(c0), jnp.max(c1)) > SB_SKIP_LOG2)

            def body(st, qheads=qheads):
                j, c0, c1, a0, a1 = st
                r0 = pl.multiple_of(j * T, T)
                keys = k_ref[0, pl.ds(r0, T), :]
                vts = vt_ref[:, pl.ds(r0, T)]
                new = []
                for hh, (c, a) in enumerate(((c0, a0), (c1, a1))):
                    z = lax.dot_general(keys, qheads[hh], dn, preferred_element_type=F32)
                    ls, later, cs = _sb_tile_logs(z, None, ut)
                    wb = _sb_tile_weights(ls, later, c, None)
                    new.append((c + cs, a + jnp.dot(vts, wb, preferred_element_type=F32)))
                return (j - 1, new[0][0], new[1][0], new[0][1], new[1][1])

            st = lax.while_loop(cond, body, (
                jnp.int32(qi - 2), car_ref[pl.ds(2 * qi, 1), :], car_ref[pl.ds(2 * qi + 1, 1), :],
                acc_ref[qi, 0], acc_ref[qi, 1]))
            emit(qi, (st[3], st[4]))


def _stick_breaking(proj3, ut, T=ATT_TILE):
    b, s, _ = proj3.shape
    kern = functools.partial(_sb_kernel, T=T, S=s)
    col = lambda c: pl.BlockSpec((1, s, LANES), lambda bi, h: (bi, 0, c + h))
    return pl.pallas_call(
        kern,
        grid=(b, SB_HEADS // 2),
        in_specs=[
            col(COL_SB_Q), col(COL_SB_K), col(COL_SB_V),
            pl.BlockSpec((T + SUBLANES, T), lambda bi, h: (0, 0)),
        ],
        out_specs=pl.BlockSpec((1, s, LANES), lambda bi, h: (bi, 0, h)),
        out_shape=jax.ShapeDtypeStruct((b, s, MIX_WIDTH), BF16),
        scratch_shapes=[pltpu.VMEM((s // T, 2, LANES, T), F32), pltpu.VMEM((2 * (s // T), T), F32),
                        pltpu.VMEM((LANES, s), BF16)],
        compiler_params=_cparams(2, VMEM_LIMIT),
        name="stick_breaking",
    )(proj3, proj3, proj3, ut)


def _hg_stage_gates(fp, qp, lb, cs_lhs):
    L = HG_BLOCK
    f = lb + (1.0 - lb) * _sigmoid(fp)
    lf = jnp.log(f) * LOG2E
    kk = 1.0 - f
    qq = qp * _sigmoid(qp)
    hi = lf.astype(BF16)
    lo = (lf - hi.astype(F32)).astype(BF16)
    cc = jnp.dot(cs_lhs, jnp.concatenate([hi, lo], axis=1), preferred_element_type=F32)
    cum = cc[:L, :LANES] + cc[:L, LANES:]
    tot = cc[L:, :LANES] + cc[L:, LANES:]
    ck = cum - jnp.log(kk) * LOG2E
    return qq, cum, tot, ck


def _hg_stage_pairs(qq, cum, ck, ones2, sub, rows):
    C = HG_SUB
    pieces, meta = [], []
    for rs in rows:
        s = rs % C
        base = rs - s
        ck_s = ck[rs:rs + 1, :]
        for tv in range(s // SUBLANES, C // SUBLANES):
            r8 = base + tv * SUBLANES
            dlt = cum[r8:r8 + SUBLANES] - ck_s
            if tv == s // SUBLANES:
                dlt = jnp.where(sub >= (s % SUBLANES), dlt, -jnp.inf)
            pieces.append(qq[r8:r8 + SUBLANES] * jnp.exp2(dlt))
            meta.append((r8, rs))
    half = len(pieces) // 2
    lhs = jnp.concatenate([jnp.concatenate(pieces[:half], axis=0),
                           jnp.concatenate(pieces[half:], axis=0)], axis=1).astype(BF16)
    return jnp.dot(lhs, ones2, preferred_element_type=F32), meta


def _hg_stage_intra(red, meta, vv, intra):
    half = len(meta) // 2
    for n, (r8, rs) in enumerate(meta):
        r = red[(n % half) * SUBLANES:(n % half + 1) * SUBLANES,
                (n // half) * LANES:(n // half + 1) * LANES]
        term = r * vv[rs:rs + 1, :]
        intra[r8] = term if r8 not in intra else intra[r8] + term
    return intra


def _hg_kernel(f_ref, i_ref, q_ref, g_ref, lb_ref, ng_ref, cs_ref, ones_ref, o_ref, row_ref, *,
               S, NH, SPLIT):
    L, C, ST = HG_BLOCK, HG_SUB, HG_STEP
    ng = ng_ref[...]
    cs_lhs = cs_ref[...]
    ones2 = ones_ref[...]
    sub = lax.broadcasted_iota(jnp.int32, (SUBLANES, LANES), 0)
    same_step = (lax.broadcasted_iota(jnp.int32, (L, L), 0) // ST
                 == lax.broadcasted_iota(jnp.int32, (L, L), 1) // ST)
    heads = range(NH)
    lanes = [slice(hd * LANES, (hd + 1) * LANES) for hd in heads]

    def block(bi, sts):
        r0 = pl.multiple_of(bi * L, L)
        gates = [_hg_stage_gates(f_ref[0, pl.ds(r0, L), ls].astype(F32),
                                 q_ref[0, pl.ds(r0, L), ls].astype(F32), lb_ref[:, ls], cs_lhs)
                 for ls in lanes]
        vbs = [i_ref[0, pl.ds(r0, L), ls] for ls in lanes]
        qhs = [(qq * jnp.exp2(cum)).astype(BF16) for qq, cum, _, _ in gates]
        kds = [jnp.exp2(tot - ck).astype(BF16) for _, _, tot, ck in gates]
        for hd in heads:
            row_ref[hd, 0] = gates[hd][3]
            row_ref[hd, 1] = vbs[hd].astype(F32)
        reds_a = [_hg_stage_pairs(gates[hd][0], gates[hd][1], row_ref.at[hd, 0], ones2, sub,
                                  range(0, SPLIT)) for hd in heads]
        uts = [[lax.dot_general(vbs[hd][i * ST:(i + 1) * ST], kds[hd][i * ST:(i + 1) * ST],
                                (((0,), (0,)), ((), ())), preferred_element_type=F32)
                for hd in heads] for i in range(L // ST)]
        cross = []
        for qq, cum, _, ck in gates:
            qb, ka = [], []
            for a0 in range(0, L, ST):
                b0 = a0 + C
                tot_a = cum[b0 - 1:b0, :]
                qb += [jnp.zeros((C, LANES), F32), qq[b0:b0 + C] * jnp.exp2(cum[b0:b0 + C] - tot_a)]
                ka += [jnp.exp2(tot_a - ck[a0:a0 + C]), jnp.zeros((C, LANES), F32)]
            sc = lax.dot_general(jnp.concatenate(qb, axis=0).astype(BF16),
                                 jnp.concatenate(ka, axis=0).astype(BF16),
                                 (((1,), (1,)), ((), ())), preferred_element_type=F32)
            cross.append(jnp.where(same_step, sc, 0.0).astype(BF16))
        sts = list(sts)
        inters = [[] for _ in heads]
        for i in range(L // ST):
            sl = slice(i * ST, (i + 1) * ST)
            for hd in heads:
                inters[hd].append(lax.dot_general(qhs[hd][sl], sts[hd].astype(BF16),
                                                  (((1,), (1,)), ((), ())),
                                                  preferred_element_type=F32))
                sts[hd] = sts[hd] * jnp.exp2(gates[hd][2][i * ST:i * ST + 1, :]) + uts[i][hd]
        offs = [jnp.dot(cross[hd], vbs[hd], preferred_element_type=F32) for hd in heads]
        reds_b = [_hg_stage_pairs(gates[hd][0], gates[hd][1], row_ref.at[hd, 0], ones2, sub,
                                  range(SPLIT, L)) for hd in heads]
        ys = []
        for hd in heads:
            vv = row_ref.at[hd, 1]
            intra = _hg_stage_intra(*reds_a[hd], vv, {})
            intra = _hg_stage_intra(*reds_b[hd], vv, intra)
            o = jnp.concatenate(inters[hd], axis=0) + offs[hd] + jnp.concatenate(
                [intra[r8] for r8 in range(0, L, SUBLANES)], axis=0)
            gp = g_ref[0, pl.ds(r0, L), lanes[hd]].astype(F32)
            ys.append(_rms(o, ng) * (gp * _sigmoid(gp)))
        o_ref[0, pl.ds(r0, L), :] = jnp.concatenate(ys, axis=1).astype(o_ref.dtype)
        return tuple(sts)

    lax.fori_loop(0, S // L, block, tuple(jnp.zeros((LANES, LANES), F32) for _ in heads))


def _hgrn2(proj3, lb_row, ng_row, cs_lhs, ones2, nh=HG_HEADS):
    b, s, _ = proj3.shape
    kern = functools.partial(_hg_kernel, S=s, NH=nh, SPLIT=HG_PAIR_SPLIT)
    w = nh * LANES
    col = lambda c: pl.BlockSpec((1, s, w), lambda bi, h: (bi, 0, c // nh + h))
    return pl.pallas_call(
        kern,
        grid=(b, HG_HEADS // nh),
        in_specs=[
            col(COL_HG_F), col(COL_HG_I), col(COL_HG_Q), col(COL_HG_G),
            pl.BlockSpec((1, w), lambda bi, h: (0, h)),
            pl.BlockSpec((1, LANES), lambda bi, h: (0, 0)),
            pl.BlockSpec((2 * HG_BLOCK, HG_BLOCK), lambda bi, h: (0, 0)),
            pl.BlockSpec((2 * LANES, 2 * LANES), lambda bi, h: (0, 0)),
        ],
        out_specs=pl.BlockSpec((1, s, w), lambda bi, h: (bi, 0, h)),
        out_shape=jax.ShapeDtypeStruct((b, s, MIX_WIDTH), BF16),
        scratch_shapes=[pltpu.VMEM((nh, 2, HG_BLOCK, LANES), F32)],
        compiler_params=_cparams(2, VMEM_LIMIT),
        name="hgrn2",
    )(proj3, proj3, proj3, proj3, lb_row, ng_row, cs_lhs, ones2)


def _mix_ffn_kernel(*refs, final):
    (yda_ref, yhg_ref, ysb_ref, g0_ref, g1_ref, g2_ref, wup_ref, wout_ref, x_ref,
     gf_ref, w1_ref, w2_ref) = refs[:12]
    fg_ref = refs[12] if final else None
    o_ref, m_ref, h_ref, u_ref = refs[-4:]
    ys = (yda_ref, yhg_ref, ysb_ref)
    gs = (g0_ref, g1_ref, g2_ref)
    nc = 256

    def lift(n):
        return [jnp.dot(ys[br][...], wup_ref[br, :, n * nc:(n + 1) * nc], preferred_element_type=F32)
                for br in range(N_BRANCH)]

    ups = lift(0)
    for n in range(D_MODEL // nc):
        nxt = lift(n + 1) if (n + 1) * nc < D_MODEL else None
        acc = None
        for br in range(N_BRANCH):
            term = _sigmoid(gs[br][:, n * nc:(n + 1) * nc].astype(F32)) * ups[br]
            acc = term if acc is None else acc + term
        m_ref[:, n * nc:(n + 1) * nc] = acc.astype(BF16)
        ups = nxt
    x1 = x_ref[...] + jnp.dot(m_ref[...], wout_ref[...], preferred_element_type=F32)
    h_ref[...] = _rms(x1, gf_ref[...]).astype(BF16)
    o_ref[...] = x1

    nf = FFN_CHUNK

    def up(c):
        h = h_ref[...]
        return (jnp.dot(h, w1_ref[:, c * nf:(c + 1) * nf], preferred_element_type=F32),
                jnp.dot(h, w1_ref[:, FFN_HIDDEN + c * nf:FFN_HIDDEN + (c + 1) * nf],
                        preferred_element_type=F32))

    ab = up(0)
    for c in range(FFN_HIDDEN // nf):
        nxt = up(c + 1) if (c + 1) * nf < FFN_HIDDEN else None
        a, bv = ab
        u_ref[:, c * nf:(c + 1) * nf] = ((a * _sigmoid(a)) * bv).astype(BF16)
        ab = nxt
    x2 = o_ref[...] + jnp.dot(u_ref[...], w2_ref[...], preferred_element_type=F32)
    o_ref[...] = _rms(x2, fg_ref[...]) if final else x2


def _mix_ffn(y_da, y_hg, y_sb, proj, w_up, w_out, x2d, g_ffn, w1, w2, layer, final_g=None, tm=512):
    m, d = x2d.shape
    tm = min(tm, m)
    final = final_g is not None
    ysp = pl.BlockSpec((tm, MIX_WIDTH), lambda i: (i, 0))
    gate = lambda br: pl.BlockSpec((tm, d), lambda i: (i, COL_GATE * LANES // D_MODEL + br))
    in_specs = [
        ysp, ysp, ysp, gate(0), gate(1), gate(2),
        _resident((N_BRANCH, MIX_WIDTH, d), layer),
        _resident((d, d), layer),
        pl.BlockSpec((tm, d), lambda i: (i, 0)),
        _resident((1, d), layer),
        _resident((d, 2 * FFN_HIDDEN), layer),
        _resident((FFN_HIDDEN, d), layer),
    ]
    args = [y_da, y_hg, y_sb, proj, proj, proj, w_up, w_out, x2d, g_ffn, w1, w2]
    if final:
        in_specs.append(_resident((1, d)))
        args.append(final_g)
    return pl.pallas_call(
        functools.partial(_mix_ffn_kernel, final=final),
        grid=(m // tm,),
        in_specs=in_specs,
        out_specs=pl.BlockSpec((tm, d), lambda i: (i, 0)),
        out_shape=jax.ShapeDtypeStruct((m, d), F32),
        scratch_shapes=[pltpu.VMEM((tm, d), BF16), pltpu.VMEM((tm, d), BF16),
                        pltpu.VMEM((tm, FFN_HIDDEN), BF16)],
        compiler_params=_cparams(1, VMEM_LIMIT),
        name="mix_ffn_final" if final else "mix_ffn",
    )(*args)


def _t5_bucket_ids(rel):
    half = REL_BUCKETS // 2
    max_exact = half // 2
    ret = (rel > 0).astype(jnp.int32) * half
    n = jnp.abs(rel)
    nf = jnp.maximum(n, 1).astype(F32)
    large = max_exact + (jnp.log(nf / max_exact) / math.log(REL_MAX_DIST / max_exact)
                         * (half - max_exact)).astype(jnp.int32)
    large = jnp.minimum(large, half - 1)
    return ret + jnp.where(n < max_exact, n, large)


def _bias_tiles(rel_bias, T):
    assert T >= REL_MAX_DIST
    rel0 = jnp.arange(T, dtype=jnp.int32)[:, None] - jnp.arange(T, dtype=jnp.int32)[None, :]
    ids = _t5_bucket_ids(jnp.stack([rel0, rel0 - T, rel0 - 2 * T]))[None]
    table = rel_bias.astype(F32).T * LOG2E
    tiles = jnp.zeros((table.shape[0],) + ids.shape[1:], F32)
    for bkt in range(REL_BUCKETS):
        tiles = jnp.where(ids == bkt, table[:, bkt][:, None, None, None], tiles)
    return tiles


def _hg_cumsum_lhs():
    r = np.arange(HG_BLOCK)[:, None]
    c = np.arange(HG_BLOCK)[None, :]
    same = (r // HG_STEP) == (c // HG_STEP)
    return jnp.asarray(np.concatenate([same & (c <= r), same], axis=0).astype(np.float32), BF16)


def _sb_later_matrix(T):
    s = np.arange(T)[:, None]
    j = np.arange(T)[None, :]
    later = (j > s).astype(np.float32)
    return jnp.asarray(np.concatenate([later, np.ones((SUBLANES, T), np.float32)], axis=0), BF16)


def _setup(b, s, norm_mix_g, w_in, rel_bias, diff_lambda, diff_subln_g, hgrn_lb_logits, hgrn_norm_g,
           w_up, w_out, norm_ffn_g, w_ffn_in, w_ffn_out):
    depth, d = w_in.shape[0], w_in.shape[1]
    T = min(ATT_TILE, s)
    lam_init = jnp.asarray([0.8 - 0.6 * math.exp(-0.3 * l) for l in range(depth)], F32)
    dl = diff_lambda.astype(F32)
    lam = (jnp.exp(jnp.sum(dl[:, 0] * dl[:, 1], axis=-1))
           - jnp.exp(jnp.sum(dl[:, 2] * dl[:, 3], axis=-1)) + lam_init)
    lam_rows = jnp.broadcast_to(lam[:, None, None], (depth, 1, T))
    da_g = diff_subln_g.astype(F32) * (1.0 - lam_init)[:, None]
    da_g = jnp.broadcast_to(da_g[:, :, None], (depth, LANES, T))
    lb_all = jnp.cumsum(jax.nn.softmax(hgrn_lb_logits.astype(F32), axis=0), axis=0)
    lb = (lb_all - lb_all[0:1])[:, None, :]

    col_scale = np.ones((IN_WIDTH,), np.float32)
    col_scale[COL_DA_Q * LANES:COL_DA_K * LANES] = LOG2E * DA_QK_DIM ** -0.5
    col_scale[COL_SB_Q * LANES:COL_SB_K * LANES] = LOG2E * SB_DIM ** -0.5
    w_in_s = (w_in.astype(F32) * jnp.asarray(col_scale)).astype(BF16)

    layers = dict(
        g_mix=norm_mix_g.astype(F32)[:, None, :],
        w_in=w_in_s,
        lam=lam_rows,
        da_g=da_g,
        lb=lb,
        hg_g=hgrn_norm_g.astype(F32)[:, None, :],
        w_up=w_up.astype(BF16),
        w_out=w_out.astype(BF16),
        g_ffn=norm_ffn_g.astype(F32)[:, None, :],
        w1=w_ffn_in.astype(BF16),
        w2=w_ffn_out.astype(BF16),
    )
    ones2 = jnp.asarray(np.kron(np.eye(2, dtype=np.float32), np.ones((LANES, LANES), np.float32)), BF16)
    consts = dict(bias=_bias_tiles(rel_bias, T), cs=_hg_cumsum_lhs(), ones2=ones2,
                  u=_sb_later_matrix(T), T=T)
    return layers, consts


def kernel(x, norm_mix_g, w_in, rel_bias, diff_lambda, diff_subln_g, hgrn_lb_logits, hgrn_norm_g,
           w_up, w_out, norm_ffn_g, w_ffn_in, w_ffn_out, final_norm_g):
    b, s, d = x.shape
    m = b * s
    layers, consts = _setup(b, s, norm_mix_g, w_in, rel_bias, diff_lambda, diff_subln_g,
                            hgrn_lb_logits, hgrn_norm_g, w_up, w_out, norm_ffn_g, w_ffn_in, w_ffn_out)
    T = consts["T"]

    depth = w_in.shape[0]
    x2d = x.astype(F32).reshape(m, d)
    p = layers
    for l in range(depth):
        proj = _inproj(x2d, p["g_mix"], p["w_in"], l)
        proj3 = proj.reshape(b, s, IN_WIDTH)
        y_da = _diff_attn(proj3, consts["bias"], p["lam"][l], p["da_g"][l], T)
        y_hg = _hgrn2(proj3, p["lb"][l], p["hg_g"][l], consts["cs"], consts["ones2"])
        y_sb = _stick_breaking(proj3, consts["u"], T)
        final_g = final_norm_g.astype(F32)[None, :] if l == depth - 1 else None
        x2d = _mix_ffn(y_da.reshape(m, MIX_WIDTH), y_hg.reshape(m, MIX_WIDTH),
                       y_sb.reshape(m, MIX_WIDTH), proj, p["w_up"], p["w_out"], x2d,
                       p["g_ffn"], p["w1"], p["w2"], l, final_g)
    return x2d.reshape(b, s, d).astype(x.dtype)
```

```python
import functools
import math

import numpy as np
import jax
import jax.numpy as jnp
from jax import lax
from jax.experimental import pallas as pl
from jax.experimental.pallas import tpu as pltpu

F32 = jnp.float32
BF16 = jnp.bfloat16

D_MODEL = 1024
MIX_WIDTH = D_MODEL // 2
N_BRANCH = 3
DA_QK_DIM = 64
DA_HEADS = 4
HG_HEADS = 4
SB_DIM = 64
SB_HEADS = 8
CHUNK = 64
REL_BUCKETS = 32
REL_MAX_DIST = 128
FFN_HIDDEN = 2816
IN_WIDTH = 10 * MIX_WIDTH + N_BRANCH * D_MODEL
NORM_EPS = 1e-6
LOG2E = math.log2(math.e)

LANES = 128
SUBLANES = 8
COL_DA_Q, COL_DA_K, COL_DA_V = 0, 4, 8
COL_HG_F, COL_HG_I, COL_HG_Q, COL_HG_G = 12, 16, 20, 24
COL_SB_Q, COL_SB_K, COL_SB_V = 28, 32, 36
COL_GATE = 40

ATT_TILE = 256
HG_BLOCK = 256
HG_SUB = 16
HG_STEP = 32
HG_PAIR_SPLIT = 16
FFN_CHUNK = 256
VMEM_LIMIT = 56 * 1024 * 1024


def _cparams(n_axes, vmem=None):
    return pltpu.CompilerParams(
        dimension_semantics=("arbitrary",) * n_axes,
        vmem_limit_bytes=vmem,
    )


def _rms(x, g):
    ms = jnp.mean(x * x, axis=-1, keepdims=True)
    return x * lax.rsqrt(ms + NORM_EPS) * g


def _sigmoid(x):
    return 1.0 / (1.0 + jnp.exp(-x))


def _fold8(x, op):
    return op(x.reshape(x.shape[0] // SUBLANES, SUBLANES, x.shape[1]), axis=0)


def _resident(shape, layer=None):
    if layer is None:
        return pl.BlockSpec(shape, lambda *_: (0,) * len(shape), pipeline_mode=pl.Buffered(1))
    return pl.BlockSpec((None,) + tuple(shape), lambda *_: (layer,) + (0,) * len(shape),
                        pipeline_mode=pl.Buffered(1))


def _inproj_kernel(x_ref, g_ref, w_ref, o_ref, *, tn):
    h = _rms(x_ref[...], g_ref[...]).astype(BF16)
    for n in range(w_ref.shape[1] // tn):
        cs = slice(n * tn, (n + 1) * tn)
        o_ref[:, cs] = jnp.dot(h, w_ref[:, cs], preferred_element_type=F32).astype(o_ref.dtype)


def _inproj(x2d, g, w, layer, tm=512, tn=1024):
    m, d = x2d.shape
    n = w.shape[-1]
    tm = min(tm, m)
    return pl.pallas_call(
        functools.partial(_inproj_kernel, tn=tn),
        grid=(m // tm,),
        in_specs=[
            pl.BlockSpec((tm, d), lambda i: (i, 0)),
            _resident((1, d), layer),
            _resident((d, n), layer),
        ],
        out_specs=pl.BlockSpec((tm, n), lambda i: (i, 0)),
        out_shape=jax.ShapeDtypeStruct((m, n), BF16),
        compiler_params=_cparams(1, VMEM_LIMIT),
        name="inproj",
    )(x2d, g, w)


DA_STAGE_LAGS = (2, 3)
DA_KEY_ROWS = 256


def _da_kernel(q_ref, k_ref, v_ref, bias_ref, lam_ref, g_ref, o_ref, s_ref, p_ref, vt_ref, *, T, S):
    row = lax.broadcasted_iota(jnp.int32, (T, T), 0)
    col = lax.broadcasted_iota(jnp.int32, (T, T), 1)
    chunk_mask = (row // CHUNK) <= (col // CHUNK)
    lane = lax.broadcasted_iota(jnp.int32, (T, LANES), 1)
    vt_ref[...] = v_ref[0].astype(F32).T.astype(BF16)
    lam = lam_ref[...]
    g = g_ref[...]
    far_bias = bias_ref[0, 2, 0:1, :]
    nq = S // T
    R = min(DA_KEY_ROWS, T)

    def scores(qi, c, slot):
        q = q_ref[0, qi * T:(qi + 1) * T, :]
        qmap = jnp.where((lane >= DA_QK_DIM) == (c == 1), q, jnp.zeros_like(q))
        mx_near, mx_far = None, None
        for r0 in range(0, (qi + 1) * T, R):
            j, off = divmod(r0, T)
            sl = slice(r0, r0 + R)
            sj = lax.dot_general(k_ref[0, sl, :], qmap, (((1,), (1,)), ((), ())),
                                 preferred_element_type=F32)
            if qi - j < 2:
                sj = sj + bias_ref[0, qi - j, off:off + R, :]
            if j == qi:
                sj = jnp.where(chunk_mask[off:off + R], sj, -jnp.inf)
            s_ref[slot, sl, :] = sj
            t8 = _fold8(sj, jnp.max)
            if qi - j < 2:
                mx_near = t8 if mx_near is None else jnp.maximum(mx_near, t8)
            else:
                mx_far = t8 if mx_far is None else jnp.maximum(mx_far, t8)
        m = jnp.max(mx_near, axis=0, keepdims=True)
        if mx_far is None:
            return m, m
        m = jnp.maximum(m, jnp.max(mx_far, axis=0, keepdims=True) + far_bias)
        return m, m - far_bias

    def probs(qi, slot, m, m_far):
        ls = None
        for r0 in range(0, (qi + 1) * T, R):
            j = r0 // T
            sl = slice(r0, r0 + R)
            p = jnp.exp2(s_ref[slot, sl, :] - (m if qi - j < 2 else m_far))
            t8 = _fold8(p, jnp.sum)
            ls = t8 if ls is None else ls + t8
            p_ref[slot, sl, :] = p.astype(BF16)
        return jnp.sum(ls, axis=0, keepdims=True)

    def values(qi, slot, l):
        nk = (qi + 1) * T
        acc = jnp.dot(vt_ref[:, :nk], p_ref[slot, :nk, :], preferred_element_type=F32)
        return acc * (1.0 / l)

    items = [(qi, c) for qi in range(nq) for c in range(2)]
    d1, d2 = DA_STAGE_LAGS
    nb = s_ref.shape[0]
    stats, sums, res = {}, {}, {}
    for step in range(len(items) + d2):
        if step < len(items):
            stats[step] = scores(*items[step], step % nb)
        n = step - d1
        if 0 <= n < len(items):
            sums[n] = probs(items[n][0], n % nb, *stats.pop(n))
        n = step - d2
        if 0 <= n < len(items):
            qi, c = items[n]
            res[c] = values(qi, n % nb, sums.pop(n))
            if c == 1:
                o = res[0] - lam * res[1]
                ms = jnp.mean(o * o, axis=0, keepdims=True)
                y = o * lax.rsqrt(ms + NORM_EPS) * g
                o_ref[0, qi * T:(qi + 1) * T, :] = y.T.astype(o_ref.dtype)


def _diff_attn(proj3, bias_tbl, lam_row, g_col, T=ATT_TILE):
    b, s, _ = proj3.shape
    kern = functools.partial(_da_kernel, T=T, S=s)
    col = lambda c: pl.BlockSpec((1, s, LANES), lambda h, bi: (bi, 0, c + h))
    return pl.pallas_call(
        kern,
        grid=(DA_HEADS, b),
        in_specs=[
            col(COL_DA_Q), col(COL_DA_K), col(COL_DA_V),
            pl.BlockSpec((1, 3, T, T), lambda h, bi: (h, 0, 0, 0)),
            pl.BlockSpec((1, T), lambda h, bi: (0, 0)),
            pl.BlockSpec((LANES, T), lambda h, bi: (0, 0)),
        ],
        out_specs=pl.BlockSpec((1, s, LANES), lambda h, bi: (bi, 0, h)),
        out_shape=jax.ShapeDtypeStruct((b, s, MIX_WIDTH), BF16),
        scratch_shapes=[pltpu.VMEM((DA_STAGE_LAGS[1] + 1, s, T), F32),
                        pltpu.VMEM((DA_STAGE_LAGS[1] + 1, s, T), BF16),
                        pltpu.VMEM((LANES, s), BF16)],
        compiler_params=_cparams(2, VMEM_LIMIT),
        name="diff_attn",
    )(proj3, proj3, proj3, bias_tbl, lam_row, g_col)


SB_SKIP_LOG2 = -160.0
SB_STAGE_LAGS = (2, 4)


def _sb_logsig(z, mask):
    neg_abs = lax.bitcast_convert_type(
        lax.bitcast_convert_type(z, jnp.uint32) | jnp.uint32(0x80000000), F32)
    l2 = jnp.log(1.0 + jnp.exp2(neg_abs)) * LOG2E
    ls = jnp.minimum(z, 0.0) - l2
    lk = ls - z
    return ls, (lk if mask is None else jnp.where(mask, lk, 0.0))


def _sb_tile_logs(z, mask, ut):
    T = z.shape[0]
    ls, lk = _sb_logsig(z, mask)
    sums = jnp.dot(ut, lk.astype(BF16), preferred_element_type=F32)
    return ls, sums[:T], sums[T:T + 1]


def _sb_tile_weights(ls, later, carry, mask):
    w = jnp.exp2(ls + (later if carry is None else later + carry))
    if mask is not None:
        w = jnp.where(mask, w, 0.0)
    return w.astype(BF16)


def _sb_diag_logs(z, strict, ut):
    T = z.shape[0]
    H = T // 2
    ls_top, lk_top = _sb_logsig(z[:H], strict[:H])
    ls_br, lk_br = _sb_logsig(z[H:, H:], strict[H:, H:])
    lk = jnp.concatenate(
        [lk_top, jnp.concatenate([jnp.zeros((H, H), F32), lk_br], axis=1)], axis=0)
    sums = jnp.dot(ut, lk.astype(BF16), preferred_element_type=F32)
    return (ls_top, ls_br), sums[:T], sums[T:T + 1]


def _sb_diag_weights(ls_parts, later, strict):
    ls_top, ls_br = ls_parts
    H = ls_br.shape[0]
    w_top = jnp.where(strict[:H], jnp.exp2(ls_top + later[:H]), 0.0)
    w_br = jnp.where(strict[H:, H:], jnp.exp2(ls_br + later[H:, H:]), 0.0)
    w = jnp.concatenate(
        [w_top, jnp.concatenate([jnp.zeros((H, H), F32), w_br], axis=1)], axis=0)
    return w.astype(BF16)


def _sb_kernel(q_ref, k_ref, v_ref, ut_ref, o_ref, acc_ref, car_ref, vt_ref, *, T, S):
    row = lax.broadcasted_iota(jnp.int32, (T, T), 0)
    col = lax.broadcasted_iota(jnp.int32, (T, T), 1)
    strict = row < col
    lane = lax.broadcasted_iota(jnp.int32, (T, LANES), 1)
    head_rows = lax.broadcasted_iota(jnp.int32, (LANES, T), 0) < SB_DIM
    vt_ref[...] = v_ref[0].astype(F32).T.astype(BF16)
    ut = ut_ref[...]
    dn = (((1,), (1,)), ((), ()))
    nq = S // T

    def qhead(qi, hh):
        q = q_ref[0, qi * T:(qi + 1) * T, :]
        return jnp.where((lane >= SB_DIM) == (hh == 1), q, jnp.zeros_like(q))

    def emit(qi, accs):
        o = jnp.where(head_rows, accs[0], accs[1])
        o_ref[0, qi * T:(qi + 1) * T, :] = o.T.astype(o_ref.dtype)

    items = [(qi, hh, back) for qi in range(nq) for hh in range(2)
             for back in ((0, 1) if qi else (0,))]

    def stage_scores(item):
        qi, hh, back = item
        j = qi - back
        return lax.dot_general(k_ref[0, j * T:(j + 1) * T, :], qhead(qi, hh), dn,
                               preferred_element_type=F32)

    def stage_logs(item, z):
        return _sb_diag_logs(z, strict, ut) if item[2] == 0 else _sb_tile_logs(z, None, ut)

    pend, done, far_carry = {}, {}, None

    def stage_weights(item, ls, later, cs):
        nonlocal far_carry
        qi, hh, back = item
        if back == 0:
            w = _sb_diag_weights(ls, later, strict)
            acc = jnp.dot(vt_ref[:, qi * T:(qi + 1) * T], w, preferred_element_type=F32)
            pend[(qi, hh)] = (acc, cs)
            if qi:
                return
        else:
            acc, carry = pend[(qi, hh)]
            w = _sb_tile_weights(ls, later, carry, None)
            acc = acc + jnp.dot(vt_ref[:, (qi - 1) * T:qi * T], w, preferred_element_type=F32)
            cs = carry + cs
        if qi >= 2:
            acc_ref[qi, hh] = acc
            car_ref[pl.ds(2 * qi + hh, 1), :] = cs
            far_carry = cs if far_carry is None else jnp.maximum(far_carry, cs)
        done[(qi, hh)] = acc
        if hh == 1:
            emit(qi, (done.pop((qi, 0)), done.pop((qi, 1))))

    zs, logs = {}, {}
    d1, d2 = SB_STAGE_LAGS
    for step in range(len(items) + d2):
        if step < len(items):
            zs[step] = stage_scores(items[step])
        if 0 <= step - d1 < len(items):
            logs[step - d1] = stage_logs(items[step - d1], zs.pop(step - d1))
        if 0 <= step - d2 < len(items):
            stage_weights(items[step - d2], *logs.pop(step - d2))

    if nq <= 2:
        return

    @pl.when(jnp.max(far_carry) > SB_SKIP_LOG2)
    def _():
        for qi in range(2, nq):
            qheads = (qhead(qi, 0), qhead(qi, 1))

            def cond(st):
                j, c0, c1, _, _ = st
                return (j >= 0) & (jnp.maximum(jnp.max(c0), jnp.max(c1)) > SB_SKIP_LOG2)

            def body(st, qheads=qheads):
                j, c0, c1, a0, a1 = st
                r0 = pl.multiple_of(j * T, T)
                keys = k_ref[0, pl.ds(r0, T), :]
                vts = vt_ref[:, pl.ds(r0, T)]
                new = []
                for hh, (c, a) in enumerate(((c0, a0), (c1, a1))):
                    z = lax.dot_general(keys, qheads[hh], dn, preferred_element_type=F32)
                    ls, later, cs = _sb_tile_logs(z, None, ut)
                    wb = _sb_tile_weights(ls, later, c, None)
                    new.append((c + cs, a + jnp.dot(vts, wb, preferred_element_type=F32)))
                return (j - 1, new[0][0], new[1][0], new[0][1], new[1][1])

            st = lax.while_loop(cond, body, (
                jnp.int32(qi - 2), car_ref[pl.ds(2 * qi, 1), :], car_ref[pl.ds(2 * qi + 1, 1), :],
                acc_ref[qi, 0], acc_ref[qi, 1]))
            emit(qi, (st[3], st[4]))


def _stick_breaking(proj3, ut, T=ATT_TILE):
    b, s, _ = proj3.shape
    kern = functools.partial(_sb_kernel, T=T, S=s)
    col = lambda c: pl.BlockSpec((1, s, LANES), lambda bi, h: (bi, 0, c + h))
    return pl.pallas_call(
        kern,
        grid=(b, SB_HEADS // 2),
        in_specs=[
            col(COL_SB_Q), col(COL_SB_K), col(COL_SB_V),
            pl.BlockSpec((T + SUBLANES, T), lambda bi, h: (0, 0)),
        ],
        out_specs=pl.BlockSpec((1, s, LANES), lambda bi, h: (bi, 0, h)),
        out_shape=jax.ShapeDtypeStruct((b, s, MIX_WIDTH), BF16),
        scratch_shapes=[pltpu.VMEM((s // T, 2, LANES, T), F32), pltpu.VMEM((2 * (s // T), T), F32),
                        pltpu.VMEM((LANES, s), BF16)],
        compiler_params=_cparams(2, VMEM_LIMIT),
        name="stick_breaking",
    )(proj3, proj3, proj3, ut)


def _hg_stage_gates(fp, qp, lb, cs_lhs):
    L = HG_BLOCK
    f = lb + (1.0 - lb) * _sigmoid(fp)
    lf = jnp.log(f) * LOG2E
    kk = 1.0 - f
    qq = qp * _sigmoid(qp)
    hi = lf.astype(BF16)
    lo = (lf - hi.astype(F32)).astype(BF16)
    cc = jnp.dot(cs_lhs, jnp.concatenate([hi, lo], axis=1), preferred_element_type=F32)
    cum = cc[:L, :LANES] + cc[:L, LANES:]
    tot = cc[L:, :LANES] + cc[L:, LANES:]
    ck = cum - jnp.log(kk) * LOG2E
    return qq, cum, tot, ck


def _hg_stage_pairs(qq, cum, ck, ones2, sub, rows):
    C = HG_SUB
    pieces, meta = [], []
    for rs in rows:
        s = rs % C
        base = rs - s
        ck_s = ck[rs:rs + 1, :]
        for tv in range(s // SUBLANES, C // SUBLANES):
            r8 = base + tv * SUBLANES
            dlt = cum[r8:r8 + SUBLANES] - ck_s
            if tv == s // SUBLANES:
                dlt = jnp.where(sub >= (s % SUBLANES), dlt, -jnp.inf)
            pieces.append(qq[r8:r8 + SUBLANES] * jnp.exp2(dlt))
            meta.append((r8, rs))
    half = len(pieces) // 2
    lhs = jnp.concatenate([jnp.concatenate(pieces[:half], axis=0),
                           jnp.concatenate(pieces[half:], axis=0)], axis=1).astype(BF16)
    return jnp.dot(lhs, ones2, preferred_element_type=F32), meta


def _hg_stage_intra(red, meta, vv, intra):
    half = len(meta) // 2
    for n, (r8, rs) in enumerate(meta):
        r = red[(n % half) * SUBLANES:(n % half + 1) * SUBLANES,
                (n // half) * LANES:(n // half + 1) * LANES]
        term = r * vv[rs:rs + 1, :]
        intra[r8] = term if r8 not in intra else intra[r8] + term
    return intra


def _hg_kernel(f_ref, i_ref, q_ref, g_ref, lb_ref, ng_ref, cs_ref, ones_ref, o_ref, row_ref, *,
               S, NH, SPLIT):
    L, C, ST = HG_BLOCK, HG_SUB, HG_STEP
    ng = ng_ref[...]
    cs_lhs = cs_ref[...]
    ones2 = ones_ref[...]
    sub = lax.broadcasted_iota(jnp.int32, (SUBLANES, LANES), 0)
    same_step = (lax.broadcasted_iota(jnp.int32, (L, L), 0) // ST
                 == lax.broadcasted_iota(jnp.int32, (L, L), 1) // ST)
    heads = range(NH)
    lanes = [slice(hd * LANES, (hd + 1) * LANES) for hd in heads]

    def block(bi, sts):
        r0 = pl.multiple_of(bi * L, L)
        gates = [_hg_stage_gates(f_ref[0, pl.ds(r0, L), ls].astype(F32),
                                 q_ref[0, pl.ds(r0, L), ls].astype(F32), lb_ref[:, ls], cs_lhs)
                 for ls in lanes]
        vbs = [i_ref[0, pl.ds(r0, L), ls] for ls in lanes]
        qhs = [(qq * jnp.exp2(cum)).astype(BF16) for qq, cum, _, _ in gates]
        kds = [jnp.exp2(tot - ck).astype(BF16) for _, _, tot, ck in gates]
        for hd in heads:
            row_ref[hd, 0] = gates[hd][3]
            row_ref[hd, 1] = vbs[hd].astype(F32)
        reds_a = [_hg_stage_pairs(gates[hd][0], gates[hd][1], row_ref.at[hd, 0], ones2, sub,
                                  range(0, SPLIT)) for hd in heads]
        uts = [[lax.dot_general(vbs[hd][i * ST:(i + 1) * ST], kds[hd][i * ST:(i + 1) * ST],
                                (((0,), (0,)), ((), ())), preferred_element_type=F32)
                for hd in heads] for i in range(L // ST)]
        cross = []
        for qq, cum, _, ck in gates:
            qb, ka = [], []
            for a0 in range(0, L, ST):
                b0 = a0 + C
                tot_a = cum[b0 - 1:b0, :]
                qb += [jnp.zeros((C, LANES), F32), qq[b0:b0 + C] * jnp.exp2(cum[b0:b0 + C] - tot_a)]
                ka += [jnp.exp2(tot_a - ck[a0:a0 + C]), jnp.zeros((C, LANES), F32)]
            sc = lax.dot_general(jnp.concatenate(qb, axis=0).astype(BF16),
                                 jnp.concatenate(ka, axis=0).astype(BF16),
                                 (((1,), (1,)), ((), ())), preferred_element_type=F32)
            cross.append(jnp.where(same_step, sc, 0.0).astype(BF16))
        sts = list(sts)
        inters = [[] for _ in heads]
        for i in range(L // ST):
            sl = slice(i * ST, (i + 1) * ST)
            for hd in heads:
                inters[hd].append(lax.dot_general(qhs[hd][sl], sts[hd].astype(BF16),
                                                  (((1,), (1,)), ((), ())),
                                                  preferred_element_type=F32))
                sts[hd] = sts[hd] * jnp.exp2(gates[hd][2][i * ST:i * ST + 1, :]) + uts[i][hd]
        offs = [jnp.dot(cross[hd], vbs[hd], preferred_element_type=F32) for hd in heads]
        reds_b = [_hg_stage_pairs(gates[hd][0], gates[hd][1], row_ref.at[hd, 0], ones2, sub,
                                  range(SPLIT, L)) for hd in heads]
        ys = []
        for hd in heads:
            vv = row_ref.at[hd, 1]
            intra = _hg_stage_intra(*reds_a[hd], vv, {})
            intra = _hg_stage_intra(*reds_b[hd], vv, intra)
            o = jnp.concatenate(inters[hd], axis=0) + offs[hd] + jnp.concatenate(
                [intra[r8] for r8 in range(0, L, SUBLANES)], axis=0)
            gp = g_ref[0, pl.ds(r0, L), lanes[hd]].astype(F32)
            ys.append(_rms(o, ng) * (gp * _sigmoid(gp)))
        o_ref[0, pl.ds(r0, L), :] = jnp.concatenate(ys, axis=1).astype(o_ref.dtype)
        return tuple(sts)

    lax.fori_loop(0, S // L, block, tuple(jnp.zeros((LANES, LANES), F32) for _ in heads))


def _hgrn2(proj3, lb_row, ng_row, cs_lhs, ones2, nh=HG_HEADS):
    b, s, _ = proj3.shape
    kern = functools.partial(_hg_kernel, S=s, NH=nh, SPLIT=HG_PAIR_SPLIT)
    w = nh * LANES
    col = lambda c: pl.BlockSpec((1, s, w), lambda bi, h: (bi, 0, c // nh + h))
    return pl.pallas_call(
        kern,
        grid=(b, HG_HEADS // nh),
        in_specs=[
            col(COL_HG_F), col(COL_HG_I), col(COL_HG_Q), col(COL_HG_G),
            pl.BlockSpec((1, w), lambda bi, h: (0, h)),
            pl.BlockSpec((1, LANES), lambda bi, h: (0, 0)),
            pl.BlockSpec((2 * HG_BLOCK, HG_BLOCK), lambda bi, h: (0, 0)),
            pl.BlockSpec((2 * LANES, 2 * LANES), lambda bi, h: (0, 0)),
        ],
        out_specs=pl.BlockSpec((1, s, w), lambda bi, h: (bi, 0, h)),
        out_shape=jax.ShapeDtypeStruct((b, s, MIX_WIDTH), BF16),
        scratch_shapes=[pltpu.VMEM((nh, 2, HG_BLOCK, LANES), F32)],
        compiler_params=_cparams(2, VMEM_LIMIT),
        name="hgrn2",
    )(proj3, proj3, proj3, proj3, lb_row, ng_row, cs_lhs, ones2)


def _mix_ffn_kernel(*refs, final):
    (yda_ref, yhg_ref, ysb_ref, g0_ref, g1_ref, g2_ref, wup_ref, wout_ref, x_ref,
     gf_ref, w1_ref, w2_ref) = refs[:12]
    fg_ref = refs[12] if final else None
    o_ref, m_ref, h_ref, u_ref = refs[-4:]
    ys = (yda_ref, yhg_ref, ysb_ref)
    gs = (g0_ref, g1_ref, g2_ref)
    nc = 256

    def lift(n):
        return [jnp.dot(ys[br][...], wup_ref[br, :, n * nc:(n + 1) * nc], preferred_element_type=F32)
                for br in range(N_BRANCH)]

    ups = lift(0)
    for n in range(D_MODEL // nc):
        nxt = lift(n + 1) if (n + 1) * nc < D_MODEL else None
        acc = None
        for br in range(N_BRANCH):
            term = _sigmoid(gs[br][:, n * nc:(n + 1) * nc].astype(F32)) * ups[br]
            acc = term if acc is None else acc + term
        m_ref[:, n * nc:(n + 1) * nc] = acc.astype(BF16)
        ups = nxt
    x1 = x_ref[...] + jnp.dot(m_ref[...], wout_ref[...], preferred_element_type=F32)
    h_ref[...] = _rms(x1, gf_ref[...]).astype(BF16)
    o_ref[...] = x1

    nf = FFN_CHUNK

    def up(c):
        h = h_ref[...]
        return (jnp.dot(h, w1_ref[:, c * nf:(c + 1) * nf], preferred_element_type=F32),
                jnp.dot(h, w1_ref[:, FFN_HIDDEN + c * nf:FFN_HIDDEN + (c + 1) * nf],
                        preferred_element_type=F32))

    ab = up(0)
    for c in range(FFN_HIDDEN // nf):
        nxt = up(c + 1) if (c + 1) * nf < FFN_HIDDEN else None
        a, bv = ab
        u_ref[:, c * nf:(c + 1) * nf] = ((a * _sigmoid(a)) * bv).astype(BF16)
        ab = nxt
    x2 = o_ref[...] + jnp.dot(u_ref[...], w2_ref[...], preferred_element_type=F32)
    o_ref[...] = _rms(x2, fg_ref[...]) if final else x2


def _mix_ffn(y_da, y_hg, y_sb, proj, w_up, w_out, x2d, g_ffn, w1, w2, layer, final_g=None, tm=512):
    m, d = x2d.shape
    tm = min(tm, m)
    final = final_g is not None
    ysp = pl.BlockSpec((tm, MIX_WIDTH), lambda i: (i, 0))
    gate = lambda br: pl.BlockSpec((tm, d), lambda i: (i, COL_GATE * LANES // D_MODEL + br))
    in_specs = [
        ysp, ysp, ysp, gate(0), gate(1), gate(2),
        _resident((N_BRANCH, MIX_WIDTH, d), layer),
        _resident((d, d), layer),
        pl.BlockSpec((tm, d), lambda i: (i, 0)),
        _resident((1, d), layer),
        _resident((d, 2 * FFN_HIDDEN), layer),
        _resident((FFN_HIDDEN, d), layer),
    ]
    args = [y_da, y_hg, y_sb, proj, proj, proj, w_up, w_out, x2d, g_ffn, w1, w2]
    if final:
        in_specs.append(_resident((1, d)))
        args.append(final_g)
    return pl.pallas_call(
        functools.partial(_mix_ffn_kernel, final=final),
        grid=(m // tm,),
        in_specs=in_specs,
        out_specs=pl.BlockSpec((tm, d), lambda i: (i, 0)),
        out_shape=jax.ShapeDtypeStruct((m, d), F32),
        scratch_shapes=[pltpu.VMEM((tm, d), BF16), pltpu.VMEM((tm, d), BF16),
                        pltpu.VMEM((tm, FFN_HIDDEN), BF16)],
        compiler_params=_cparams(1, VMEM_LIMIT),
        name="mix_ffn_final" if final else "mix_ffn",
    )(*args)


def _t5_bucket_ids(rel):
    half = REL_BUCKETS // 2
    max_exact = half // 2
    ret = (rel > 0).astype(jnp.int32) * half
    n = jnp.abs(rel)
    nf = jnp.maximum(n, 1).astype(F32)
    large = max_exact + (jnp.log(nf / max_exact) / math.log(REL_MAX_DIST / max_exact)
                         * (half - max_exact)).astype(jnp.int32)
    large = jnp.minimum(large, half - 1)
    return ret + jnp.where(n < max_exact, n, large)


def _bias_tiles(rel_bias, T):
    assert T >= REL_MAX_DIST
    rel0 = jnp.arange(T, dtype=jnp.int32)[:, None] - jnp.arange(T, dtype=jnp.int32)[None, :]
    ids = _t5_bucket_ids(jnp.stack([rel0, rel0 - T, rel0 - 2 * T]))[None]
    table = rel_bias.astype(F32).T * LOG2E
    tiles = jnp.zeros((table.shape[0],) + ids.shape[1:], F32)
    for bkt in range(REL_BUCKETS):
        tiles = jnp.where(ids == bkt, table[:, bkt][:, None, None, None], tiles)
    return tiles


def _hg_cumsum_lhs():
    r = np.arange(HG_BLOCK)[:, None]
    c = np.arange(HG_BLOCK)[None, :]
    same = (r // HG_STEP) == (c // HG_STEP)
    return jnp.asarray(np.concatenate([same & (c <= r), same], axis=0).astype(np.float32), BF16)


def _sb_later_matrix(T):
    s = np.arange(T)[:, None]
    j = np.arange(T)[None, :]
    later = (j > s).astype(np.float32)
    return jnp.asarray(np.concatenate([later, np.ones((SUBLANES, T), np.float32)], axis=0), BF16)


def _setup(b, s, norm_mix_g, w_in, rel_bias, diff_lambda, diff_subln_g, hgrn_lb_logits, hgrn_norm_g,
           w_up, w_out, norm_ffn_g, w_ffn_in, w_ffn_out):
    depth, d = w_in.shape[0], w_in.shape[1]
    T = min(ATT_TILE, s)
    lam_init = jnp.asarray([0.8 - 0.6 * math.exp(-0.3 * l) for l in range(depth)], F32)
    dl = diff_lambda.astype(F32)
    lam = (jnp.exp(jnp.sum(dl[:, 0] * dl[:, 1], axis=-1))
           - jnp.exp(jnp.sum(dl[:, 2] * dl[:, 3], axis=-1)) + lam_init)
    lam_rows = jnp.broadcast_to(lam[:, None, None], (depth, 1, T))
    da_g = diff_subln_g.astype(F32) * (1.0 - lam_init)[:, None]
    da_g = jnp.broadcast_to(da_g[:, :, None], (depth, LANES, T))
    lb_all = jnp.cumsum(jax.nn.softmax(hgrn_lb_logits.astype(F32), axis=0), axis=0)
    lb = (lb_all - lb_all[0:1])[:, None, :]

    col_scale = np.ones((IN_WIDTH,), np.float32)
    col_scale[COL_DA_Q * LANES:COL_DA_K * LANES] = LOG2E * DA_QK_DIM ** -0.5
    col_scale[COL_SB_Q * LANES:COL_SB_K * LANES] = LOG2E * SB_DIM ** -0.5
    w_in_s = (w_in.astype(F32) * jnp.asarray(col_scale)).astype(BF16)

    layers = dict(
        g_mix=norm_mix_g.astype(F32)[:, None, :],
        w_in=w_in_s,
        lam=lam_rows,
        da_g=da_g,
        lb=lb,
        hg_g=hgrn_norm_g.astype(F32)[:, None, :],
        w_up=w_up.astype(BF16),
        w_out=w_out.astype(BF16),
        g_ffn=norm_ffn_g.astype(F32)[:, None, :],
        w1=w_ffn_in.astype(BF16),
        w2=w_ffn_out.astype(BF16),
    )
    ones2 = jnp.asarray(np.kron(np.eye(2, dtype=np.float32), np.ones((LANES, LANES), np.float32)), BF16)
    consts = dict(bias=_bias_tiles(rel_bias, T), cs=_hg_cumsum_lhs(), ones2=ones2,
                  u=_sb_later_matrix(T), T=T)
    return layers, consts


def kernel(x, norm_mix_g, w_in, rel_bias, diff_lambda, diff_subln_g, hgrn_lb_logits, hgrn_norm_g,
           w_up, w_out, norm_ffn_g, w_ffn_in, w_ffn_out, final_norm_g):
    b, s, d = x.shape
    m = b * s
    layers, consts = _setup(b, s, norm_mix_g, w_in, rel_bias, diff_lambda, diff_subln_g,
                            hgrn_lb_logits, hgrn_norm_g, w_up, w_out, norm_ffn_g, w_ffn_in, w_ffn_out)
    T = consts["T"]

    depth = w_in.shape[0]
    x2d = x.astype(F32).reshape(m, d)
    p = layers
    for l in range(depth):
        proj = _inproj(x2d, p["g_mix"], p["w_in"], l)
        proj3 = proj.reshape(b, s, IN_WIDTH)
        y_da = _diff_attn(proj3, consts["bias"], p["lam"][l], p["da_g"][l], T)
        y_hg = _hgrn2(proj3, p["lb"][l], p["hg_g"][l], consts["cs"], consts["ones2"])
        y_sb = _stick_breaking(proj3, consts["u"], T)
        final_g = final_norm_g.astype(F32)[None, :] if l == depth - 1 else None
        x2d = _mix_ffn(y_da.reshape(m, MIX_WIDTH), y_hg.reshape(m, MIX_WIDTH),
                       y_sb.reshape(m, MIX_WIDTH), proj, p["w_up"], p["w_out"], x2d,
                       p["g_ffn"], p["w1"], p["w2"], l, final_g)
    return x2d.reshape(b, s, d).astype(x.dtype)
```

```python
import functools
import math

import numpy as np
import jax
import jax.numpy as jnp
from jax import lax
from jax.experimental import pallas as pl
from jax.experimental.pallas import tpu as pltpu

F32 = jnp.float32
BF16 = jnp.bfloat16

D_MODEL = 1024
MIX_WIDTH = D_MODEL // 2
N_BRANCH = 3
DA_QK_DIM = 64
DA_HEADS = 4
HG_HEADS = 4
SB_DIM = 64
SB_HEADS = 8
CHUNK = 64
REL_BUCKETS = 32
REL_MAX_DIST = 128
FFN_HIDDEN = 2816
IN_WIDTH = 10 * MIX_WIDTH + N_BRANCH * D_MODEL
NORM_EPS = 1e-6
LOG2E = math.log2(math.e)

LANES = 128
SUBLANES = 8
COL_DA_Q, COL_DA_K, COL_DA_V = 0, 4, 8
COL_HG_F, COL_HG_I, COL_HG_Q, COL_HG_G = 12, 16, 20, 24
COL_SB_Q, COL_SB_K, COL_SB_V = 28, 32, 36
COL_GATE = 40

ATT_TILE = 256
HG_BLOCK = 256
HG_SUB = 16
HG_STEP = 32
HG_PAIR_SPLIT = 16
FFN_CHUNK = 256
VMEM_LIMIT = 56 * 1024 * 1024


def _cparams(n_axes, vmem=None):
    return pltpu.CompilerParams(
        dimension_semantics=("arbitrary",) * n_axes,
        vmem_limit_bytes=vmem,
    )


def _rms(x, g):
    ms = jnp.mean(x * x, axis=-1, keepdims=True)
    return x * lax.rsqrt(ms + NORM_EPS) * g


def _sigmoid(x):
    return 1.0 / (1.0 + jnp.exp(-x))


def _fold8(x, op):
    return op(x.reshape(x.shape[0] // SUBLANES, SUBLANES, x.shape[1]), axis=0)


def _resident(shape, layer=None):
    if layer is None:
        return pl.BlockSpec(shape, lambda *_: (0,) * len(shape), pipeline_mode=pl.Buffered(1))
    return pl.BlockSpec((None,) + tuple(shape), lambda *_: (layer,) + (0,) * len(shape),
                        pipeline_mode=pl.Buffered(1))


def _inproj_kernel(x_ref, g_ref, w_ref, o_ref, *, tn):
    h = _rms(x_ref[...], g_ref[...]).astype(BF16)
    for n in range(w_ref.shape[1] // tn):
        cs = slice(n * tn, (n + 1) * tn)
        o_ref[:, cs] = jnp.dot(h, w_ref[:, cs], preferred_element_type=F32).astype(o_ref.dtype)


def _inproj(x2d, g, w, layer, tm=512, tn=1024):
    m, d = x2d.shape
    n = w.shape[-1]
    tm = min(tm, m)
    return pl.pallas_call(
        functools.partial(_inproj_kernel, tn=tn),
        grid=(m // tm,),
        in_specs=[
            pl.BlockSpec((tm, d), lambda i: (i, 0)),
            _resident((1, d), layer),
            _resident((d, n), layer),
        ],
        out_specs=pl.BlockSpec((tm, n), lambda i: (i, 0)),
        out_shape=jax.ShapeDtypeStruct((m, n), BF16),
        compiler_params=_cparams(1, VMEM_LIMIT),
        name="inproj",
    )(x2d, g, w)


DA_STAGE_LAGS = (1, 2)
DA_KEY_ROWS = 256


def _da_kernel(q_ref, k_ref, v_ref, bias_ref, lam_ref, g_ref, o_ref, s_ref, p_ref, vt_ref, *, T, S):
    row = lax.broadcasted_iota(jnp.int32, (T, T), 0)
    col = lax.broadcasted_iota(jnp.int32, (T, T), 1)
    chunk_mask = (row // CHUNK) <= (col // CHUNK)
    lane = lax.broadcasted_iota(jnp.int32, (T, LANES), 1)
    vt_ref[...] = v_ref[0].astype(F32).T.astype(BF16)
    lam = lam_ref[...]
    g = g_ref[...]
    far_bias = bias_ref[0, 2, 0:1, :]
    nq = S // T
    R = min(DA_KEY_ROWS, T)

    def scores(qi, c, slot):
        q = q_ref[0, qi * T:(qi + 1) * T, :]
        qmap = jnp.where((lane >= DA_QK_DIM) == (c == 1), q, jnp.zeros_like(q))
        mx_near, mx_far = None, None
        for r0 in range(0, (qi + 1) * T, R):
            j, off = divmod(r0, T)
            sl = slice(r0, r0 + R)
            sj = lax.dot_general(k_ref[0, sl, :], qmap, (((1,), (1,)), ((), ())),
                                 preferred_element_type=F32)
            if qi - j < 2:
                sj = sj + bias_ref[0, qi - j, off:off + R, :]
            if j == qi:
                sj = jnp.where(chunk_mask[off:off + R], sj, -jnp.inf)
            s_ref[slot, sl, :] = sj
            t8 = _fold8(sj, jnp.max)
            if qi - j < 2:
                mx_near = t8 if mx_near is None else jnp.maximum(mx_near, t8)
            else:
                mx_far = t8 if mx_far is None else jnp.maximum(mx_far, t8)
        m = jnp.max(mx_near, axis=0, keepdims=True)
        if mx_far is None:
            return m, m
        m = jnp.maximum(m, jnp.max(mx_far, axis=0, keepdims=True) + far_bias)
        return m, m - far_bias

    def probs(qi, slot, m, m_far):
        ls = None
        for r0 in range(0, (qi + 1) * T, R):
            j = r0 // T
            sl = slice(r0, r0 + R)
            p = jnp.exp2(s_ref[slot, sl, :] - (m if qi - j < 2 else m_far))
            t8 = _fold8(p, jnp.sum)
            ls = t8 if ls is None else ls + t8
            p_ref[slot, sl, :] = p.astype(BF16)
        return jnp.sum(ls, axis=0, keepdims=True)

    def values(qi, slot, l):
        nk = (qi + 1) * T
        acc = jnp.dot(vt_ref[:, :nk], p_ref[slot, :nk, :], preferred_element_type=F32)
        return acc * (1.0 / l)

    items = [(qi, c) for qi in range(nq) for c in range(2)]
    d1, d2 = DA_STAGE_LAGS
    nb = s_ref.shape[0]
    stats, sums, res = {}, {}, {}
    for step in range(len(items) + d2):
        if step < len(items):
            stats[step] = scores(*items[step], step % nb)
        n = step - d1
        if 0 <= n < len(items):
            sums[n] = probs(items[n][0], n % nb, *stats.pop(n))
        n = step - d2
        if 0 <= n < len(items):
            qi, c = items[n]
            res[c] = values(qi, n % nb, sums.pop(n))
            if c == 1:
                o = res[0] - lam * res[1]
                ms = jnp.mean(o * o, axis=0, keepdims=True)
                y = o * lax.rsqrt(ms + NORM_EPS) * g
                o_ref[0, qi * T:(qi + 1) * T, :] = y.T.astype(o_ref.dtype)


def _diff_attn(proj3, bias_tbl, lam_row, g_col, T=ATT_TILE):
    b, s, _ = proj3.shape
    kern = functools.partial(_da_kernel, T=T, S=s)
    col = lambda c: pl.BlockSpec((1, s, LANES), lambda h, bi: (bi, 0, c + h))
    return pl.pallas_call(
        kern,
        grid=(DA_HEADS, b),
        in_specs=[
            col(COL_DA_Q), col(COL_DA_K), col(COL_DA_V),
            pl.BlockSpec((1, 3, T, T), lambda h, bi: (h, 0, 0, 0)),
            pl.BlockSpec((1, T), lambda h, bi: (0, 0)),
            pl.BlockSpec((LANES, T), lambda h, bi: (0, 0)),
        ],
        out_specs=pl.BlockSpec((1, s, LANES), lambda h, bi: (bi, 0, h)),
        out_shape=jax.ShapeDtypeStruct((b, s, MIX_WIDTH), BF16),
        scratch_shapes=[pltpu.VMEM((DA_STAGE_LAGS[1] + 1, s, T), F32),
                        pltpu.VMEM((DA_STAGE_LAGS[1] + 1, s, T), BF16),
                        pltpu.VMEM((LANES, s), BF16)],
        compiler_params=_cparams(2, VMEM_LIMIT),
        name="diff_attn",
    )(proj3, proj3, proj3, bias_tbl, lam_row, g_col)


SB_SKIP_LOG2 = -160.0
SB_STAGE_LAGS = (2, 4)


def _sb_logsig(z, mask):
    neg_abs = lax.bitcast_convert_type(
        lax.bitcast_convert_type(z, jnp.uint32) | jnp.uint32(0x80000000), F32)
    l2 = jnp.log(1.0 + jnp.exp2(neg_abs)) * LOG2E
    ls = jnp.minimum(z, 0.0) - l2
    lk = ls - z
    return ls, (lk if mask is None else jnp.where(mask, lk, 0.0))


def _sb_tile_logs(z, mask, ut):
    T = z.shape[0]
    ls, lk = _sb_logsig(z, mask)
    sums = jnp.dot(ut, lk.astype(BF16), preferred_element_type=F32)
    return ls, sums[:T], sums[T:T + 1]


def _sb_tile_weights(ls, later, carry, mask):
    w = jnp.exp2(ls + (later if carry is None else later + carry))
    if mask is not None:
        w = jnp.where(mask, w, 0.0)
    return w.astype(BF16)


def _sb_diag_logs(z, strict, ut):
    T = z.shape[0]
    H = T // 2
    ls_top, lk_top = _sb_logsig(z[:H], strict[:H])
    ls_br, lk_br = _sb_logsig(z[H:, H:], strict[H:, H:])
    lk = jnp.concatenate(
        [lk_top, jnp.concatenate([jnp.zeros((H, H), F32), lk_br], axis=1)], axis=0)
    sums = jnp.dot(ut, lk.astype(BF16), preferred_element_type=F32)
    return (ls_top, ls_br), sums[:T], sums[T:T + 1]


def _sb_diag_weights(ls_parts, later, strict):
    ls_top, ls_br = ls_parts
    H = ls_br.shape[0]
    w_top = jnp.where(strict[:H], jnp.exp2(ls_top + later[:H]), 0.0)
    w_br = jnp.where(strict[H:, H:], jnp.exp2(ls_br + later[H:, H:]), 0.0)
    w = jnp.concatenate(
        [w_top, jnp.concatenate([jnp.zeros((H, H), F32), w_br], axis=1)], axis=0)
    return w.astype(BF16)


def _sb_kernel(q_ref, k_ref, v_ref, ut_ref, o_ref, acc_ref, car_ref, vt_ref, *, T, S):
    row = lax.broadcasted_iota(jnp.int32, (T, T), 0)
    col = lax.broadcasted_iota(jnp.int32, (T, T), 1)
    strict = row < col
    lane = lax.broadcasted_iota(jnp.int32, (T, LANES), 1)
    head_rows = lax.broadcasted_iota(jnp.int32, (LANES, T), 0) < SB_DIM
    vt_ref[...] = v_ref[0].astype(F32).T.astype(BF16)
    ut = ut_ref[...]
    dn = (((1,), (1,)), ((), ()))
    nq = S // T

    def qhead(qi, hh):
        q = q_ref[0, qi * T:(qi + 1) * T, :]
        return jnp.where((lane >= SB_DIM) == (hh == 1), q, jnp.zeros_like(q))

    def emit(qi, accs):
        o = jnp.where(head_rows, accs[0], accs[1])
        o_ref[0, qi * T:(qi + 1) * T, :] = o.T.astype(o_ref.dtype)

    items = [(qi, hh, back) for qi in range(nq) for hh in range(2)
             for back in ((0, 1) if qi else (0,))]

    def stage_scores(item):
        qi, hh, back = item
        j = qi - back
        return lax.dot_general(k_ref[0, j * T:(j + 1) * T, :], qhead(qi, hh), dn,
                               preferred_element_type=F32)

    def stage_logs(item, z):
        return _sb_diag_logs(z, strict, ut) if item[2] == 0 else _sb_tile_logs(z, None, ut)

    pend, done, far_carry = {}, {}, None

    def stage_weights(item, ls, later, cs):
        nonlocal far_carry
        qi, hh, back = item
        if back == 0:
            w = _sb_diag_weights(ls, later, strict)
            acc = jnp.dot(vt_ref[:, qi * T:(qi + 1) * T], w, preferred_element_type=F32)
            pend[(qi, hh)] = (acc, cs)
            if qi:
                return
        else:
            acc, carry = pend[(qi, hh)]
            w = _sb_tile_weights(ls, later, carry, None)
            acc = acc + jnp.dot(vt_ref[:, (qi - 1) * T:qi * T], w, preferred_element_type=F32)
            cs = carry + cs
        if qi >= 2:
            acc_ref[qi, hh] = acc
            car_ref[pl.ds(2 * qi + hh, 1), :] = cs
            far_carry = cs if far_carry is None else jnp.maximum(far_carry, cs)
        done[(qi, hh)] = acc
        if hh == 1:
            emit(qi, (done.pop((qi, 0)), done.pop((qi, 1))))

    zs, logs = {}, {}
    d1, d2 = SB_STAGE_LAGS
    for step in range(len(items) + d2):
        if step < len(items):
            zs[step] = stage_scores(items[step])
        if 0 <= step - d1 < len(items):
            logs[step - d1] = stage_logs(items[step - d1], zs.pop(step - d1))
        if 0 <= step - d2 < len(items):
            stage_weights(items[step - d2], *logs.pop(step - d2))

    if nq <= 2:
        return

    @pl.when(jnp.max(far_carry) > SB_SKIP_LOG2)
    def _():
        for qi in range(2, nq):
            qheads = (qhead(qi, 0), qhead(qi, 1))

            def cond(st):
                j, c0, c1, _, _ = st
                return (j >= 0) & (jnp.maximum(jnp.max(c0), jnp.max(c1)) > SB_SKIP_LOG2)

            def body(st, qheads=qheads):
                j, c0, c1, a0, a1 = st
                r0 = pl.multiple_of(j * T, T)
                keys = k_ref[0, pl.ds(r0, T), :]
                vts = vt_ref[:, pl.ds(r0, T)]
                new = []
                for hh, (c, a) in enumerate(((c0, a0), (c1, a1))):
                    z = lax.dot_general(keys, qheads[hh], dn, preferred_element_type=F32)
                    ls, later, cs = _sb_tile_logs(z, None, ut)
                    wb = _sb_tile_weights(ls, later, c, None)
                    new.append((c + cs, a + jnp.dot(vts, wb, preferred_element_type=F32)))
                return (j - 1, new[0][0], new[1][0], new[0][1], new[1][1])

            st = lax.while_loop(cond, body, (
                jnp.int32(qi - 2), car_ref[pl.ds(2 * qi, 1), :], car_ref[pl.ds(2 * qi + 1, 1), :],
                acc_ref[qi, 0], acc_ref[qi, 1]))
            emit(qi, (st[3], st[4]))


def _stick_breaking(proj3, ut, T=ATT_TILE):
    b, s, _ = proj3.shape
    kern = functools.partial(_sb_kernel, T=T, S=s)
    col = lambda c: pl.BlockSpec((1, s, LANES), lambda bi, h: (bi, 0, c + h))
    return pl.pallas_call(
        kern,
        grid=(b, SB_HEADS // 2),
        in_specs=[
            col(COL_SB_Q), col(COL_SB_K), col(COL_SB_V),
            pl.BlockSpec((T + SUBLANES, T), lambda bi, h: (0, 0)),
        ],
        out_specs=pl.BlockSpec((1, s, LANES), lambda bi, h: (bi, 0, h)),
        out_shape=jax.ShapeDtypeStruct((b, s, MIX_WIDTH), BF16),
        scratch_shapes=[pltpu.VMEM((s // T, 2, LANES, T), F32), pltpu.VMEM((2 * (s // T), T), F32),
                        pltpu.VMEM((LANES, s), BF16)],
        compiler_params=_cparams(2, VMEM_LIMIT),
        name="stick_breaking",
    )(proj3, proj3, proj3, ut)


def _hg_stage_gates(fp, qp, lb, cs_lhs):
    L = HG_BLOCK
    f = lb + (1.0 - lb) * _sigmoid(fp)
    lf = jnp.log(f) * LOG2E
    kk = 1.0 - f
    qq = qp * _sigmoid(qp)
    hi = lf.astype(BF16)
    lo = (lf - hi.astype(F32)).astype(BF16)
    cc = jnp.dot(cs_lhs, jnp.concatenate([hi, lo], axis=1), preferred_element_type=F32)
    cum = cc[:L, :LANES] + cc[:L, LANES:]
    tot = cc[L:, :LANES] + cc[L:, LANES:]
    ck = cum - jnp.log(kk) * LOG2E
    return qq, cum, tot, ck


def _hg_stage_pairs(qq, cum, ck, ones2, sub, rows):
    C = HG_SUB
    pieces, meta = [], []
    for rs in rows:
        s = rs % C
        base = rs - s
        ck_s = ck[rs:rs + 1, :]
        for tv in range(s // SUBLANES, C // SUBLANES):
            r8 = base + tv * SUBLANES
            dlt = cum[r8:r8 + SUBLANES] - ck_s
            if tv == s // SUBLANES:
                dlt = jnp.where(sub >= (s % SUBLANES), dlt, -jnp.inf)
            pieces.append(qq[r8:r8 + SUBLANES] * jnp.exp2(dlt))
            meta.append((r8, rs))
    half = len(pieces) // 2
    lhs = jnp.concatenate([jnp.concatenate(pieces[:half], axis=0),
                           jnp.concatenate(pieces[half:], axis=0)], axis=1).astype(BF16)
    return jnp.dot(lhs, ones2, preferred_element_type=F32), meta


def _hg_stage_intra(red, meta, vv, intra):
    half = len(meta) // 2
    for n, (r8, rs) in enumerate(meta):
        r = red[(n % half) * SUBLANES:(n % half + 1) * SUBLANES,
                (n // half) * LANES:(n // half + 1) * LANES]
        term = r * vv[rs:rs + 1, :]
        intra[r8] = term if r8 not in intra else intra[r8] + term
    return intra


def _hg_kernel(f_ref, i_ref, q_ref, g_ref, lb_ref, ng_ref, cs_ref, ones_ref, o_ref, row_ref, *,
               S, NH, SPLIT):
    L, C, ST = HG_BLOCK, HG_SUB, HG_STEP
    ng = ng_ref[...]
    cs_lhs = cs_ref[...]
    ones2 = ones_ref[...]
    sub = lax.broadcasted_iota(jnp.int32, (SUBLANES, LANES), 0)
    same_step = (lax.broadcasted_iota(jnp.int32, (L, L), 0) // ST
                 == lax.broadcasted_iota(jnp.int32, (L, L), 1) // ST)
    heads = range(NH)
    lanes = [slice(hd * LANES, (hd + 1) * LANES) for hd in heads]

    def block(bi, sts):
        r0 = pl.multiple_of(bi * L, L)
        gates = [_hg_stage_gates(f_ref[0, pl.ds(r0, L), ls].astype(F32),
                                 q_ref[0, pl.ds(r0, L), ls].astype(F32), lb_ref[:, ls], cs_lhs)
                 for ls in lanes]
        vbs = [i_ref[0, pl.ds(r0, L), ls] for ls in lanes]
        qhs = [(qq * jnp.exp2(cum)).astype(BF16) for qq, cum, _, _ in gates]
        kds = [jnp.exp2(tot - ck).astype(BF16) for _, _, tot, ck in gates]
        for hd in heads:
            row_ref[hd, 0] = gates[hd][3]
            row_ref[hd, 1] = vbs[hd].astype(F32)
        reds_a = [_hg_stage_pairs(gates[hd][0], gates[hd][1], row_ref.at[hd, 0], ones2, sub,
                                  range(0, SPLIT)) for hd in heads]
        uts = [[lax.dot_general(vbs[hd][i * ST:(i + 1) * ST], kds[hd][i * ST:(i + 1) * ST],
                                (((0,), (0,)), ((), ())), preferred_element_type=F32)
                for hd in heads] for i in range(L // ST)]
        cross = []
        for qq, cum, _, ck in gates:
            qb, ka = [], []
            for a0 in range(0, L, ST):
                b0 = a0 + C
                tot_a = cum[b0 - 1:b0, :]
                qb += [jnp.zeros((C, LANES), F32), qq[b0:b0 + C] * jnp.exp2(cum[b0:b0 + C] - tot_a)]
                ka += [jnp.exp2(tot_a - ck[a0:a0 + C]), jnp.zeros((C, LANES), F32)]
            sc = lax.dot_general(jnp.concatenate(qb, axis=0).astype(BF16),
                                 jnp.concatenate(ka, axis=0).astype(BF16),
                                 (((1,), (1,)), ((), ())), preferred_element_type=F32)
            cross.append(jnp.where(same_step, sc, 0.0).astype(BF16))
        sts = list(sts)
        inters = [[] for _ in heads]
        for i in range(L // ST):
            sl = slice(i * ST, (i + 1) * ST)
            for hd in heads:
                inters[hd].append(lax.dot_general(qhs[hd][sl], sts[hd].astype(BF16),
                                                  (((1,), (1,)), ((), ())),
                                                  preferred_element_type=F32))
                sts[hd] = sts[hd] * jnp.exp2(gates[hd][2][i * ST:i * ST + 1, :]) + uts[i][hd]
        offs = [jnp.dot(cross[hd], vbs[hd], preferred_element_type=F32) for hd in heads]
        reds_b = [_hg_stage_pairs(gates[hd][0], gates[hd][1], row_ref.at[hd, 0], ones2, sub,
                                  range(SPLIT, L)) for hd in heads]
        ys = []
        for hd in heads:
            vv = row_ref.at[hd, 1]
            intra = _hg_stage_intra(*reds_a[hd], vv, {})
            intra = _hg_stage_intra(*reds_b[hd], vv, intra)
            o = jnp.concatenate(inters[hd], axis=0) + offs[hd] + jnp.concatenate(
                [intra[r8] for r8 in range(0, L, SUBLANES)], axis=0)
            gp = g_ref[0, pl.ds(r0, L), lanes[hd]].astype(F32)
            ys.append(_rms(o, ng) * (gp * _sigmoid(gp)))
        o_ref[0, pl.ds(r0, L), :] = jnp.concatenate(ys, axis=1).astype(o_ref.dtype)
        return tuple(sts)

    lax.fori_loop(0, S // L, block, tuple(jnp.zeros((LANES, LANES), F32) for _ in heads))


def _hgrn2(proj3, lb_row, ng_row, cs_lhs, ones2, nh=HG_HEADS):
    b, s, _ = proj3.shape
    kern = functools.partial(_hg_kernel, S=s, NH=nh, SPLIT=HG_PAIR_SPLIT)
    w = nh * LANES
    col = lambda c: pl.BlockSpec((1, s, w), lambda bi, h: (bi, 0, c // nh + h))
    return pl.pallas_call(
        kern,
        grid=(b, HG_HEADS // nh),
        in_specs=[
            col(COL_HG_F), col(COL_HG_I), col(COL_HG_Q), col(COL_HG_G),
            pl.BlockSpec((1, w), lambda bi, h: (0, h)),
            pl.BlockSpec((1, LANES), lambda bi, h: (0, 0)),
            pl.BlockSpec((2 * HG_BLOCK, HG_BLOCK), lambda bi, h: (0, 0)),
            pl.BlockSpec((2 * LANES, 2 * LANES), lambda bi, h: (0, 0)),
        ],
        out_specs=pl.BlockSpec((1, s, w), lambda bi, h: (bi, 0, h)),
        out_shape=jax.ShapeDtypeStruct((b, s, MIX_WIDTH), BF16),
        scratch_shapes=[pltpu.VMEM((nh, 2, HG_BLOCK, LANES), F32)],
        compiler_params=_cparams(2, VMEM_LIMIT),
        name="hgrn2",
    )(proj3, proj3, proj3, proj3, lb_row, ng_row, cs_lhs, ones2)


def _mix_ffn_kernel(*refs, final):
    (yda_ref, yhg_ref, ysb_ref, g0_ref, g1_ref, g2_ref, wup_ref, wout_ref, x_ref,
     gf_ref, w1_ref, w2_ref) = refs[:12]
    fg_ref = refs[12] if final else None
    o_ref, m_ref, h_ref, u_ref = refs[-4:]
    ys = (yda_ref, yhg_ref, ysb_ref)
    gs = (g0_ref, g1_ref, g2_ref)
    nc = 256

    def lift(n):
        return [jnp.dot(ys[br][...], wup_ref[br, :, n * nc:(n + 1) * nc], preferred_element_type=F32)
                for br in range(N_BRANCH)]

    ups = lift(0)
    for n in range(D_MODEL // nc):
        nxt = lift(n + 1) if (n + 1) * nc < D_MODEL else None
        acc = None
        for br in range(N_BRANCH):
            term = _sigmoid(gs[br][:, n * nc:(n + 1) * nc].astype(F32)) * ups[br]
            acc = term if acc is None else acc + term
        m_ref[:, n * nc:(n + 1) * nc] = acc.astype(BF16)
        ups = nxt
    x1 = x_ref[...] + jnp.dot(m_ref[...], wout_ref[...], preferred_element_type=F32)
    h_ref[...] = _rms(x1, gf_ref[...]).astype(BF16)
    o_ref[...] = x1

    nf = FFN_CHUNK

    def up(c):
        h = h_ref[...]
        return (jnp.dot(h, w1_ref[:, c * nf:(c + 1) * nf], preferred_element_type=F32),
                jnp.dot(h, w1_ref[:, FFN_HIDDEN + c * nf:FFN_HIDDEN + (c + 1) * nf],
                        preferred_element_type=F32))

    ab = up(0)
    for c in range(FFN_HIDDEN // nf):
        nxt = up(c + 1) if (c + 1) * nf < FFN_HIDDEN else None
        a, bv = ab
        u_ref[:, c * nf:(c + 1) * nf] = ((a * _sigmoid(a)) * bv).astype(BF16)
        ab = nxt
    x2 = o_ref[...] + jnp.dot(u_ref[...], w2_ref[...], preferred_element_type=F32)
    o_ref[...] = _rms(x2, fg_ref[...]) if final else x2


def _mix_ffn(y_da, y_hg, y_sb, proj, w_up, w_out, x2d, g_ffn, w1, w2, layer, final_g=None, tm=512):
    m, d = x2d.shape
    tm = min(tm, m)
    final = final_g is not None
    ysp = pl.BlockSpec((tm, MIX_WIDTH), lambda i: (i, 0))
    gate = lambda br: pl.BlockSpec((tm, d), lambda i: (i, COL_GATE * LANES // D_MODEL + br))
    in_specs = [
        ysp, ysp, ysp, gate(0), gate(1), gate(2),
        _resident((N_BRANCH, MIX_WIDTH, d), layer),
        _resident((d, d), layer),
        pl.BlockSpec((tm, d), lambda i: (i, 0)),
        _resident((1, d), layer),
        _resident((d, 2 * FFN_HIDDEN), layer),
        _resident((FFN_HIDDEN, d), layer),
    ]
    args = [y_da, y_hg, y_sb, proj, proj, proj, w_up, w_out, x2d, g_ffn, w1, w2]
    if final:
        in_specs.append(_resident((1, d)))
        args.append(final_g)
    return pl.pallas_call(
        functools.partial(_mix_ffn_kernel, final=final),
        grid=(m // tm,),
        in_specs=in_specs,
        out_specs=pl.BlockSpec((tm, d), lambda i: (i, 0)),
        out_shape=jax.ShapeDtypeStruct((m, d), F32),
        scratch_shapes=[pltpu.VMEM((tm, d), BF16), pltpu.VMEM((tm, d), BF16),
                        pltpu.VMEM((tm, FFN_HIDDEN), BF16)],
        compiler_params=_cparams(1, VMEM_LIMIT),
        name="mix_ffn_final" if final else "mix_ffn",
    )(*args)


def _t5_bucket_ids(rel):
    half = REL_BUCKETS // 2
    max_exact = half // 2
    ret = (rel > 0).astype(jnp.int32) * half
    n = jnp.abs(rel)
    nf = jnp.maximum(n, 1).astype(F32)
    large = max_exact + (jnp.log(nf / max_exact) / math.log(REL_MAX_DIST / max_exact)
                         * (half - max_exact)).astype(jnp.int32)
    large = jnp.minimum(large, half - 1)
    return ret + jnp.where(n < max_exact, n, large)


def _bias_tiles(rel_bias, T):
    assert T >= REL_MAX_DIST
    rel0 = jnp.arange(T, dtype=jnp.int32)[:, None] - jnp.arange(T, dtype=jnp.int32)[None, :]
    ids = _t5_bucket_ids(jnp.stack([rel0, rel0 - T, rel0 - 2 * T]))[None]
    table = rel_bias.astype(F32).T * LOG2E
    tiles = jnp.zeros((table.shape[0],) + ids.shape[1:], F32)
    for bkt in range(REL_BUCKETS):
        tiles = jnp.where(ids == bkt, table[:, bkt][:, None, None, None], tiles)
    return tiles


def _hg_cumsum_lhs():
    r = np.arange(HG_BLOCK)[:, None]
    c = np.arange(HG_BLOCK)[None, :]
    same = (r // HG_STEP) == (c // HG_STEP)
    return jnp.asarray(np.concatenate([same & (c <= r), same], axis=0).astype(np.float32), BF16)


def _sb_later_matrix(T):
    s = np.arange(T)[:, None]
    j = np.arange(T)[None, :]
    later = (j > s).astype(np.float32)
    return jnp.asarray(np.concatenate([later, np.ones((SUBLANES, T), np.float32)], axis=0), BF16)


def _setup(b, s, norm_mix_g, w_in, rel_bias, diff_lambda, diff_subln_g, hgrn_lb_logits, hgrn_norm_g,
           w_up, w_out, norm_ffn_g, w_ffn_in, w_ffn_out):
    depth, d = w_in.shape[0], w_in.shape[1]
    T = min(ATT_TILE, s)
    lam_init = jnp.asarray([0.8 - 0.6 * math.exp(-0.3 * l) for l in range(depth)], F32)
    dl = diff_lambda.astype(F32)
    lam = (jnp.exp(jnp.sum(dl[:, 0] * dl[:, 1], axis=-1))
           - jnp.exp(jnp.sum(dl[:, 2] * dl[:, 3], axis=-1)) + lam_init)
    lam_rows = jnp.broadcast_to(lam[:, None, None], (depth, 1, T))
    da_g = diff_subln_g.astype(F32) * (1.0 - lam_init)[:, None]
    da_g = jnp.broadcast_to(da_g[:, :, None], (depth, LANES, T))
    lb_all = jnp.cumsum(jax.nn.softmax(hgrn_lb_logits.astype(F32), axis=0), axis=0)
    lb = (lb_all - lb_all[0:1])[:, None, :]

    col_scale = np.ones((IN_WIDTH,), np.float32)
    col_scale[COL_DA_Q * LANES:COL_DA_K * LANES] = LOG2E * DA_QK_DIM ** -0.5
    col_scale[COL_SB_Q * LANES:COL_SB_K * LANES] = LOG2E * SB_DIM ** -0.5
    w_in_s = (w_in.astype(F32) * jnp.asarray(col_scale)).astype(BF16)

    layers = dict(
        g_mix=norm_mix_g.astype(F32)[:, None, :],
        w_in=w_in_s,
        lam=lam_rows,
        da_g=da_g,
        lb=lb,
        hg_g=hgrn_norm_g.astype(F32)[:, None, :],
        w_up=w_up.astype(BF16),
        w_out=w_out.astype(BF16),
        g_ffn=norm_ffn_g.astype(F32)[:, None, :],
        w1=w_ffn_in.astype(BF16),
        w2=w_ffn_out.astype(BF16),
    )
    ones2 = jnp.asarray(np.kron(np.eye(2, dtype=np.float32), np.ones((LANES, LANES), np.float32)), BF16)
    consts = dict(bias=_bias_tiles(rel_bias, T), cs=_hg_cumsum_lhs(), ones2=ones2,
                  u=_sb_later_matrix(T), T=T)
    return layers, consts


def kernel(x, norm_mix_g, w_in, rel_bias, diff_lambda, diff_subln_g, hgrn_lb_logits, hgrn_norm_g,
           w_up, w_out, norm_ffn_g, w_ffn_in, w_ffn_out, final_norm_g):
    b, s, d = x.shape
    m = b * s
    layers, consts = _setup(b, s, norm_mix_g, w_in, rel_bias, diff_lambda, diff_subln_g,
                            hgrn_lb_logits, hgrn_norm_g, w_up, w_out, norm_ffn_g, w_ffn_in, w_ffn_out)
    T = consts["T"]

    depth = w_in.shape[0]
    x2d = x.astype(F32).reshape(m, d)
    p = layers
    for l in range(depth):
        proj = _inproj(x2d, p["g_mix"], p["w_in"], l)
        proj3 = proj.reshape(b, s, IN_WIDTH)
        y_da = _diff_attn(proj3, consts["bias"], p["lam"][l], p["da_g"][l], T)
        y_hg = _hgrn2(proj3, p["lb"][l], p["hg_g"][l], consts["cs"], consts["ones2"])
        y_sb = _stick_breaking(proj3, consts["u"], T)
        final_g = final_norm_g.astype(F32)[None, :] if l == depth - 1 else None
        x2d = _mix_ffn(y_da.reshape(m, MIX_WIDTH), y_hg.reshape(m, MIX_WIDTH),
                       y_sb.reshape(m, MIX_WIDTH), proj, p["w_up"], p["w_out"], x2d,
                       p["g_ffn"], p["w1"], p["w2"], l, final_g)
    return x2d.reshape(b, s, d).astype(x.dtype)
```

```python
import functools
import math

import numpy as np
import jax
import jax.numpy as jnp
from jax import lax
from jax.experimental import pallas as pl
from jax.experimental.pallas import tpu as pltpu

F32 = jnp.float32
BF16 = jnp.bfloat16

D_MODEL = 1024
MIX_WIDTH = D_MODEL // 2
N_BRANCH = 3
DA_QK_DIM = 64
DA_HEADS = 4
HG_HEADS = 4
SB_DIM = 64
SB_HEADS = 8
CHUNK = 64
REL_BUCKETS = 32
REL_MAX_DIST = 128
FFN_HIDDEN = 2816
IN_WIDTH = 10 * MIX_WIDTH + N_BRANCH * D_MODEL
NORM_EPS = 1e-6
LOG2E = math.log2(math.e)

LANES = 128
SUBLANES = 8
COL_DA_Q, COL_DA_K, COL_DA_V = 0, 4, 8
COL_HG_F, COL_HG_I, COL_HG_Q, COL_HG_G = 12, 16, 20, 24
COL_SB_Q, COL_SB_K, COL_SB_V = 28, 32, 36
COL_GATE = 40

ATT_TILE = 256
HG_BLOCK = 256
HG_SUB = 16
HG_STEP = 32
HG_PAIR_SPLIT = 16
FFN_CHUNK = 256
VMEM_LIMIT = 56 * 1024 * 1024


def _cparams(n_axes, vmem=None):
    return pltpu.CompilerParams(
        dimension_semantics=("arbitrary",) * n_axes,
        vmem_limit_bytes=vmem,
    )


def _rms(x, g):
    ms = jnp.mean(x * x, axis=-1, keepdims=True)
    return x * lax.rsqrt(ms + NORM_EPS) * g


def _sigmoid(x):
    return 1.0 / (1.0 + jnp.exp(-x))


def _fold8(x, op):
    return op(x.reshape(x.shape[0] // SUBLANES, SUBLANES, x.shape[1]), axis=0)


def _resident(shape, layer=None):
    if layer is None:
        return pl.BlockSpec(shape, lambda *_: (0,) * len(shape), pipeline_mode=pl.Buffered(1))
    return pl.BlockSpec((None,) + tuple(shape), lambda *_: (layer,) + (0,) * len(shape),
                        pipeline_mode=pl.Buffered(1))


def _inproj_kernel(x_ref, g_ref, w_ref, o_ref, *, tn):
    h = _rms(x_ref[...], g_ref[...]).astype(BF16)
    for n in range(w_ref.shape[1] // tn):
        cs = slice(n * tn, (n + 1) * tn)
        o_ref[:, cs] = jnp.dot(h, w_ref[:, cs], preferred_element_type=F32).astype(o_ref.dtype)


def _inproj(x2d, g, w, layer, tm=512, tn=1024):
    m, d = x2d.shape
    n = w.shape[-1]
    tm = min(tm, m)
    return pl.pallas_call(
        functools.partial(_inproj_kernel, tn=tn),
        grid=(m // tm,),
        in_specs=[
            pl.BlockSpec((tm, d), lambda i: (i, 0)),
            _resident((1, d), layer),
            _resident((d, n), layer),
        ],
        out_specs=pl.BlockSpec((tm, n), lambda i: (i, 0)),
        out_shape=jax.ShapeDtypeStruct((m, n), BF16),
        compiler_params=_cparams(1, VMEM_LIMIT),
        name="inproj",
    )(x2d, g, w)


DA_STAGE_LAGS = (2, 3)
DA_KEY_ROWS = 256


def _da_kernel(q_ref, k_ref, v_ref, bias_ref, lam_ref, g_ref, o_ref, s_ref, p_ref, vt_ref, *, T, S):
    row = lax.broadcasted_iota(jnp.int32, (T, T), 0)
    col = lax.broadcasted_iota(jnp.int32, (T, T), 1)
    chunk_mask = (row // CHUNK) <= (col // CHUNK)
    lane = lax.broadcasted_iota(jnp.int32, (T, LANES), 1)
    vt_ref[...] = v_ref[0].astype(F32).T.astype(BF16)
    lam = lam_ref[...]
    g = g_ref[...]
    far_bias = bias_ref[0, 2, 0:1, :]
    nq = S // T
    R = min(DA_KEY_ROWS, T)

    def scores(qi, c, slot):
        q = q_ref[0, qi * T:(qi + 1) * T, :]
        qmap = jnp.where((lane >= DA_QK_DIM) == (c == 1), q, jnp.zeros_like(q))
        mx_near, mx_far = None, None
        for r0 in range(0, (qi + 1) * T, R):
            j, off = divmod(r0, T)
            sl = slice(r0, r0 + R)
            sj = lax.dot_general(k_ref[0, sl, :], qmap, (((1,), (1,)), ((), ())),
                                 preferred_element_type=F32)
            if qi - j < 2:
                sj = sj + bias_ref[0, qi - j, off:off + R, :]
            if j == qi:
                sj = jnp.where(chunk_mask[off:off + R], sj, -jnp.inf)
            s_ref[slot, sl, :] = sj
            t8 = _fold8(sj, jnp.max)
            if qi - j < 2:
                mx_near = t8 if mx_near is None else jnp.maximum(mx_near, t8)
            else:
                mx_far = t8 if mx_far is None else jnp.maximum(mx_far, t8)
        m = jnp.max(mx_near, axis=0, keepdims=True)
        if mx_far is None:
            return m, m
        m = jnp.maximum(m, jnp.max(mx_far, axis=0, keepdims=True) + far_bias)
        return m, m - far_bias

    def probs(qi, slot, m, m_far):
        ls = None
        for r0 in range(0, (qi + 1) * T, R):
            j = r0 // T
            sl = slice(r0, r0 + R)
            p = jnp.exp2(s_ref[slot, sl, :] - (m if qi - j < 2 else m_far))
            t8 = _fold8(p, jnp.sum)
            ls = t8 if ls is None else ls + t8
            p_ref[slot, sl, :] = p.astype(BF16)
        return jnp.sum(ls, axis=0, keepdims=True)

    def values(qi, slot, l):
        nk = (qi + 1) * T
        acc = lax.dot_general(v_ref[0, :nk, :], p_ref[slot, :nk, :], (((0,), (0,)), ((), ())),
                              preferred_element_type=F32)
        return acc * (1.0 / l)

    items = [(qi, c) for qi in range(nq) for c in range(2)]
    d1, d2 = DA_STAGE_LAGS
    nb = s_ref.shape[0]
    stats, sums, res = {}, {}, {}
    for step in range(len(items) + d2):
        if step < len(items):
            stats[step] = scores(*items[step], step % nb)
        n = step - d1
        if 0 <= n < len(items):
            sums[n] = probs(items[n][0], n % nb, *stats.pop(n))
        n = step - d2
        if 0 <= n < len(items):
            qi, c = items[n]
            res[c] = values(qi, n % nb, sums.pop(n))
            if c == 1:
                o = res[0] - lam * res[1]
                ms = jnp.mean(o * o, axis=0, keepdims=True)
                y = o * lax.rsqrt(ms + NORM_EPS) * g
                o_ref[0, qi * T:(qi + 1) * T, :] = y.T.astype(o_ref.dtype)


def _diff_attn(proj3, bias_tbl, lam_row, g_col, T=ATT_TILE):
    b, s, _ = proj3.shape
    kern = functools.partial(_da_kernel, T=T, S=s)
    col = lambda c: pl.BlockSpec((1, s, LANES), lambda h, bi: (bi, 0, c + h))
    return pl.pallas_call(
        kern,
        grid=(DA_HEADS, b),
        in_specs=[
            col(COL_DA_Q), col(COL_DA_K), col(COL_DA_V),
            pl.BlockSpec((1, 3, T, T), lambda h, bi: (h, 0, 0, 0)),
            pl.BlockSpec((1, T), lambda h, bi: (0, 0)),
            pl.BlockSpec((LANES, T), lambda h, bi: (0, 0)),
        ],
        out_specs=pl.BlockSpec((1, s, LANES), lambda h, bi: (bi, 0, h)),
        out_shape=jax.ShapeDtypeStruct((b, s, MIX_WIDTH), BF16),
        scratch_shapes=[pltpu.VMEM((DA_STAGE_LAGS[1] + 1, s, T), F32),
                        pltpu.VMEM((DA_STAGE_LAGS[1] + 1, s, T), BF16),
                        pltpu.VMEM((LANES, s), BF16)],
        compiler_params=_cparams(2, VMEM_LIMIT),
        name="diff_attn",
    )(proj3, proj3, proj3, bias_tbl, lam_row, g_col)


SB_SKIP_LOG2 = -160.0
SB_STAGE_LAGS = (2, 4)


def _sb_logsig(z, mask):
    neg_abs = lax.bitcast_convert_type(
        lax.bitcast_convert_type(z, jnp.uint32) | jnp.uint32(0x80000000), F32)
    l2 = jnp.log(1.0 + jnp.exp2(neg_abs)) * LOG2E
    ls = jnp.minimum(z, 0.0) - l2
    lk = ls - z
    return ls, (lk if mask is None else jnp.where(mask, lk, 0.0))


def _sb_tile_logs(z, mask, ut):
    T = z.shape[0]
    ls, lk = _sb_logsig(z, mask)
    sums = jnp.dot(ut, lk.astype(BF16), preferred_element_type=F32)
    return ls, sums[:T], sums[T:T + 1]


def _sb_tile_weights(ls, later, carry, mask):
    w = jnp.exp2(ls + (later if carry is None else later + carry))
    if mask is not None:
        w = jnp.where(mask, w, 0.0)
    return w.astype(BF16)


def _sb_diag_logs(z, strict, ut):
    T = z.shape[0]
    H = T // 2
    ls_top, lk_top = _sb_logsig(z[:H], strict[:H])
    ls_br, lk_br = _sb_logsig(z[H:, H:], strict[H:, H:])
    lk = jnp.concatenate(
        [lk_top, jnp.concatenate([jnp.zeros((H, H), F32), lk_br], axis=1)], axis=0)
    sums = jnp.dot(ut, lk.astype(BF16), preferred_element_type=F32)
    return (ls_top, ls_br), sums[:T], sums[T:T + 1]


def _sb_diag_weights(ls_parts, later, strict):
    ls_top, ls_br = ls_parts
    H = ls_br.shape[0]
    w_top = jnp.where(strict[:H], jnp.exp2(ls_top + later[:H]), 0.0)
    w_br = jnp.where(strict[H:, H:], jnp.exp2(ls_br + later[H:, H:]), 0.0)
    w = jnp.concatenate(
        [w_top, jnp.concatenate([jnp.zeros((H, H), F32), w_br], axis=1)], axis=0)
    return w.astype(BF16)


def _sb_kernel(q_ref, k_ref, v_ref, ut_ref, o_ref, acc_ref, car_ref, vt_ref, *, T, S):
    row = lax.broadcasted_iota(jnp.int32, (T, T), 0)
    col = lax.broadcasted_iota(jnp.int32, (T, T), 1)
    strict = row < col
    lane = lax.broadcasted_iota(jnp.int32, (T, LANES), 1)
    head_rows = lax.broadcasted_iota(jnp.int32, (LANES, T), 0) < SB_DIM
    vt_ref[...] = v_ref[0].astype(F32).T.astype(BF16)
    ut = ut_ref[...]
    dn = (((1,), (1,)), ((), ()))
    nq = S // T

    def qhead(qi, hh):
        q = q_ref[0, qi * T:(qi + 1) * T, :]
        return jnp.where((lane >= SB_DIM) == (hh == 1), q, jnp.zeros_like(q))

    def emit(qi, accs):
        o = jnp.where(head_rows, accs[0], accs[1])
        o_ref[0, qi * T:(qi + 1) * T, :] = o.T.astype(o_ref.dtype)

    items = [(qi, hh, back) for qi in range(nq) for hh in range(2)
             for back in ((0, 1) if qi else (0,))]

    def stage_scores(item):
        qi, hh, back = item
        j = qi - back
        return lax.dot_general(k_ref[0, j * T:(j + 1) * T, :], qhead(qi, hh), dn,
                               preferred_element_type=F32)

    def stage_logs(item, z):
        return _sb_diag_logs(z, strict, ut) if item[2] == 0 else _sb_tile_logs(z, None, ut)

    pend, done, far_carry = {}, {}, None

    def stage_weights(item, ls, later, cs):
        nonlocal far_carry
        qi, hh, back = item
        if back == 0:
            w = _sb_diag_weights(ls, later, strict)
            acc = jnp.dot(vt_ref[:, qi * T:(qi + 1) * T], w, preferred_element_type=F32)
            pend[(qi, hh)] = (acc, cs)
            if qi:
                return
        else:
            acc, carry = pend[(qi, hh)]
            w = _sb_tile_weights(ls, later, carry, None)
            acc = acc + jnp.dot(vt_ref[:, (qi - 1) * T:qi * T], w, preferred_element_type=F32)
            cs = carry + cs
        if qi >= 2:
            acc_ref[qi, hh] = acc
            car_ref[pl.ds(2 * qi + hh, 1), :] = cs
            far_carry = cs if far_carry is None else jnp.maximum(far_carry, cs)
        done[(qi, hh)] = acc
        if hh == 1:
            emit(qi, (done.pop((qi, 0)), done.pop((qi, 1))))

    zs, logs = {}, {}
    d1, d2 = SB_STAGE_LAGS
    for step in range(len(items) + d2):
        if step < len(items):
            zs[step] = stage_scores(items[step])
        if 0 <= step - d1 < len(items):
            logs[step - d1] = stage_logs(items[step - d1], zs.pop(step - d1))
        if 0 <= step - d2 < len(items):
            stage_weights(items[step - d2], *logs.pop(step - d2))

    if nq <= 2:
        return

    @pl.when(jnp.max(far_carry) > SB_SKIP_LOG2)
    def _():
        for qi in range(2, nq):
            qheads = (qhead(qi, 0), qhead(qi, 1))

            def cond(st):
                j, c0, c1, _, _ = st
                return (j >= 0) & (jnp.maximum(jnp.max(c0), jnp.max(c1)) > SB_SKIP_LOG2)

            def body(st, qheads=qheads):
                j, c0, c1, a0, a1 = st
                r0 = pl.multiple_of(j * T, T)
                keys = k_ref[0, pl.ds(r0, T), :]
                vts = vt_ref[:, pl.ds(r0, T)]
                new = []
                for hh, (c, a) in enumerate(((c0, a0), (c1, a1))):
                    z = lax.dot_general(keys, qheads[hh], dn, preferred_element_type=F32)
                    ls, later, cs = _sb_tile_logs(z, None, ut)
                    wb = _sb_tile_weights(ls, later, c, None)
                    new.append((c + cs, a + jnp.dot(vts, wb, preferred_element_type=F32)))
                return (j - 1, new[0][0], new[1][0], new[0][1], new[1][1])

            st = lax.while_loop(cond, body, (
                jnp.int32(qi - 2), car_ref[pl.ds(2 * qi, 1), :], car_ref[pl.ds(2 * qi + 1, 1), :],
                acc_ref[qi, 0], acc_ref[qi, 1]))
            emit(qi, (st[3], st[4]))


def _stick_breaking(proj3, ut, T=ATT_TILE):
    b, s, _ = proj3.shape
    kern = functools.partial(_sb_kernel, T=T, S=s)
    col = lambda c: pl.BlockSpec((1, s, LANES), lambda bi, h: (bi, 0, c + h))
    return pl.pallas_call(
        kern,
        grid=(b, SB_HEADS // 2),
        in_specs=[
            col(COL_SB_Q), col(COL_SB_K), col(COL_SB_V),
            pl.BlockSpec((T + SUBLANES, T), lambda bi, h: (0, 0)),
        ],
        out_specs=pl.BlockSpec((1, s, LANES), lambda bi, h: (bi, 0, h)),
        out_shape=jax.ShapeDtypeStruct((b, s, MIX_WIDTH), BF16),
        scratch_shapes=[pltpu.VMEM((s // T, 2, LANES, T), F32), pltpu.VMEM((2 * (s // T), T), F32),
                        pltpu.VMEM((LANES, s), BF16)],
        compiler_params=_cparams(2, VMEM_LIMIT),
        name="stick_breaking",
    )(proj3, proj3, proj3, ut)


def _hg_stage_gates(fp, qp, lb, cs_lhs):
    L = HG_BLOCK
    f = lb + (1.0 - lb) * _sigmoid(fp)
    lf = jnp.log(f) * LOG2E
    kk = 1.0 - f
    qq = qp * _sigmoid(qp)
    hi = lf.astype(BF16)
    lo = (lf - hi.astype(F32)).astype(BF16)
    cc = jnp.dot(cs_lhs, jnp.concatenate([hi, lo], axis=1), preferred_element_type=F32)
    cum = cc[:L, :LANES] + cc[:L, LANES:]
    tot = cc[L:, :LANES] + cc[L:, LANES:]
    ck = cum - jnp.log(kk) * LOG2E
    return qq, cum, tot, ck


def _hg_stage_pairs(qq, cum, ck, ones2, sub, rows):
    C = HG_SUB
    pieces, meta = [], []
    for rs in rows:
        s = rs % C
        base = rs - s
        ck_s = ck[rs:rs + 1, :]
        for tv in range(s // SUBLANES, C // SUBLANES):
            r8 = base + tv * SUBLANES
            dlt = cum[r8:r8 + SUBLANES] - ck_s
            if tv == s // SUBLANES:
                dlt = jnp.where(sub >= (s % SUBLANES), dlt, -jnp.inf)
            pieces.append(qq[r8:r8 + SUBLANES] * jnp.exp2(dlt))
            meta.append((r8, rs))
    half = len(pieces) // 2
    lhs = jnp.concatenate([jnp.concatenate(pieces[:half], axis=0),
                           jnp.concatenate(pieces[half:], axis=0)], axis=1).astype(BF16)
    return jnp.dot(lhs, ones2, preferred_element_type=F32), meta


def _hg_stage_intra(red, meta, vv, intra):
    half = len(meta) // 2
    for n, (r8, rs) in enumerate(meta):
        r = red[(n % half) * SUBLANES:(n % half + 1) * SUBLANES,
                (n // half) * LANES:(n // half + 1) * LANES]
        term = r * vv[rs:rs + 1, :]
        intra[r8] = term if r8 not in intra else intra[r8] + term
    return intra


def _hg_kernel(f_ref, i_ref, q_ref, g_ref, lb_ref, ng_ref, cs_ref, ones_ref, o_ref, row_ref, *,
               S, NH, SPLIT):
    L, C, ST = HG_BLOCK, HG_SUB, HG_STEP
    ng = ng_ref[...]
    cs_lhs = cs_ref[...]
    ones2 = ones_ref[...]
    sub = lax.broadcasted_iota(jnp.int32, (SUBLANES, LANES), 0)
    same_step = (lax.broadcasted_iota(jnp.int32, (L, L), 0) // ST
                 == lax.broadcasted_iota(jnp.int32, (L, L), 1) // ST)
    heads = range(NH)
    lanes = [slice(hd * LANES, (hd + 1) * LANES) for hd in heads]

    def block(bi, sts):
        r0 = pl.multiple_of(bi * L, L)
        gates = [_hg_stage_gates(f_ref[0, pl.ds(r0, L), ls].astype(F32),
                                 q_ref[0, pl.ds(r0, L), ls].astype(F32), lb_ref[:, ls], cs_lhs)
                 for ls in lanes]
        vbs = [i_ref[0, pl.ds(r0, L), ls] for ls in lanes]
        qhs = [(qq * jnp.exp2(cum)).astype(BF16) for qq, cum, _, _ in gates]
        kds = [jnp.exp2(tot - ck).astype(BF16) for _, _, tot, ck in gates]
        for hd in heads:
            row_ref[hd, 0] = gates[hd][3]
            row_ref[hd, 1] = vbs[hd].astype(F32)
        reds_a = [_hg_stage_pairs(gates[hd][0], gates[hd][1], row_ref.at[hd, 0], ones2, sub,
                                  range(0, SPLIT)) for hd in heads]
        uts = [[lax.dot_general(vbs[hd][i * ST:(i + 1) * ST], kds[hd][i * ST:(i + 1) * ST],
                                (((0,), (0,)), ((), ())), preferred_element_type=F32)
                for hd in heads] for i in range(L // ST)]
        cross = []
        for qq, cum, _, ck in gates:
            qb, ka = [], []
            for a0 in range(0, L, ST):
                b0 = a0 + C
                tot_a = cum[b0 - 1:b0, :]
                qb += [jnp.zeros((C, LANES), F32), qq[b0:b0 + C] * jnp.exp2(cum[b0:b0 + C] - tot_a)]
                ka += [jnp.exp2(tot_a - ck[a0:a0 + C]), jnp.zeros((C, LANES), F32)]
            sc = lax.dot_general(jnp.concatenate(qb, axis=0).astype(BF16),
                                 jnp.concatenate(ka, axis=0).astype(BF16),
                                 (((1,), (1,)), ((), ())), preferred_element_type=F32)
            cross.append(jnp.where(same_step, sc, 0.0).astype(BF16))
        sts = list(sts)
        inters = [[] for _ in heads]
        for i in range(L // ST):
            sl = slice(i * ST, (i + 1) * ST)
            for hd in heads:
                inters[hd].append(lax.dot_general(qhs[hd][sl], sts[hd].astype(BF16),
                                                  (((1,), (1,)), ((), ())),
                                                  preferred_element_type=F32))
                sts[hd] = sts[hd] * jnp.exp2(gates[hd][2][i * ST:i * ST + 1, :]) + uts[i][hd]
        offs = [jnp.dot(cross[hd], vbs[hd], preferred_element_type=F32) for hd in heads]
        reds_b = [_hg_stage_pairs(gates[hd][0], gates[hd][1], row_ref.at[hd, 0], ones2, sub,
                                  range(SPLIT, L)) for hd in heads]
        ys = []
        for hd in heads:
            vv = row_ref.at[hd, 1]
            intra = _hg_stage_intra(*reds_a[hd], vv, {})
            intra = _hg_stage_intra(*reds_b[hd], vv, intra)
            o = jnp.concatenate(inters[hd], axis=0) + offs[hd] + jnp.concatenate(
                [intra[r8] for r8 in range(0, L, SUBLANES)], axis=0)
            gp = g_ref[0, pl.ds(r0, L), lanes[hd]].astype(F32)
            ys.append(_rms(o, ng) * (gp * _sigmoid(gp)))
        o_ref[0, pl.ds(r0, L), :] = jnp.concatenate(ys, axis=1).astype(o_ref.dtype)
        return tuple(sts)

    lax.fori_loop(0, S // L, block, tuple(jnp.zeros((LANES, LANES), F32) for _ in heads))


def _hgrn2(proj3, lb_row, ng_row, cs_lhs, ones2, nh=HG_HEADS):
    b, s, _ = proj3.shape
    kern = functools.partial(_hg_kernel, S=s, NH=nh, SPLIT=HG_PAIR_SPLIT)
    w = nh * LANES
    col = lambda c: pl.BlockSpec((1, s, w), lambda bi, h: (bi, 0, c // nh + h))
    return pl.pallas_call(
        kern,
        grid=(b, HG_HEADS // nh),
        in_specs=[
            col(COL_HG_F), col(COL_HG_I), col(COL_HG_Q), col(COL_HG_G),
            pl.BlockSpec((1, w), lambda bi, h: (0, h)),
            pl.BlockSpec((1, LANES), lambda bi, h: (0, 0)),
            pl.BlockSpec((2 * HG_BLOCK, HG_BLOCK), lambda bi, h: (0, 0)),
            pl.BlockSpec((2 * LANES, 2 * LANES), lambda bi, h: (0, 0)),
        ],
        out_specs=pl.BlockSpec((1, s, w), lambda bi, h: (bi, 0, h)),
        out_shape=jax.ShapeDtypeStruct((b, s, MIX_WIDTH), BF16),
        scratch_shapes=[pltpu.VMEM((nh, 2, HG_BLOCK, LANES), F32)],
        compiler_params=_cparams(2, VMEM_LIMIT),
        name="hgrn2",
    )(proj3, proj3, proj3, proj3, lb_row, ng_row, cs_lhs, ones2)


def _mix_ffn_kernel(*refs, final):
    (yda_ref, yhg_ref, ysb_ref, g0_ref, g1_ref, g2_ref, wup_ref, wout_ref, x_ref,
     gf_ref, w1_ref, w2_ref) = refs[:12]
    fg_ref = refs[12] if final else None
    o_ref, m_ref, h_ref, u_ref = refs[-4:]
    ys = (yda_ref, yhg_ref, ysb_ref)
    gs = (g0_ref, g1_ref, g2_ref)
    nc = 256

    def lift(n):
        return [jnp.dot(ys[br][...], wup_ref[br, :, n * nc:(n + 1) * nc], preferred_element_type=F32)
                for br in range(N_BRANCH)]

    ups = lift(0)
    for n in range(D_MODEL // nc):
        nxt = lift(n + 1) if (n + 1) * nc < D_MODEL else None
        acc = None
        for br in range(N_BRANCH):
            term = _sigmoid(gs[br][:, n * nc:(n + 1) * nc].astype(F32)) * ups[br]
            acc = term if acc is None else acc + term
        m_ref[:, n * nc:(n + 1) * nc] = acc.astype(BF16)
        ups = nxt
    x1 = x_ref[...] + jnp.dot(m_ref[...], wout_ref[...], preferred_element_type=F32)
    h_ref[...] = _rms(x1, gf_ref[...]).astype(BF16)
    o_ref[...] = x1

    nf = FFN_CHUNK

    def up(c):
        h = h_ref[...]
        return (jnp.dot(h, w1_ref[:, c * nf:(c + 1) * nf], preferred_element_type=F32),
                jnp.dot(h, w1_ref[:, FFN_HIDDEN + c * nf:FFN_HIDDEN + (c + 1) * nf],
                        preferred_element_type=F32))

    ab = up(0)
    for c in range(FFN_HIDDEN // nf):
        nxt = up(c + 1) if (c + 1) * nf < FFN_HIDDEN else None
        a, bv = ab
        u_ref[:, c * nf:(c + 1) * nf] = ((a * _sigmoid(a)) * bv).astype(BF16)
        ab = nxt
    x2 = o_ref[...] + jnp.dot(u_ref[...], w2_ref[...], preferred_element_type=F32)
    o_ref[...] = _rms(x2, fg_ref[...]) if final else x2


def _mix_ffn(y_da, y_hg, y_sb, proj, w_up, w_out, x2d, g_ffn, w1, w2, layer, final_g=None, tm=512):
    m, d = x2d.shape
    tm = min(tm, m)
    final = final_g is not None
    ysp = pl.BlockSpec((tm, MIX_WIDTH), lambda i: (i, 0))
    gate = lambda br: pl.BlockSpec((tm, d), lambda i: (i, COL_GATE * LANES // D_MODEL + br))
    in_specs = [
        ysp, ysp, ysp, gate(0), gate(1), gate(2),
        _resident((N_BRANCH, MIX_WIDTH, d), layer),
        _resident((d, d), layer),
        pl.BlockSpec((tm, d), lambda i: (i, 0)),
        _resident((1, d), layer),
        _resident((d, 2 * FFN_HIDDEN), layer),
        _resident((FFN_HIDDEN, d), layer),
    ]
    args = [y_da, y_hg, y_sb, proj, proj, proj, w_up, w_out, x2d, g_ffn, w1, w2]
    if final:
        in_specs.append(_resident((1, d)))
        args.append(final_g)
    return pl.pallas_call(
        functools.partial(_mix_ffn_kernel, final=final),
        grid=(m // tm,),
        in_specs=in_specs,
        out_specs=pl.BlockSpec((tm, d), lambda i: (i, 0)),
        out_shape=jax.ShapeDtypeStruct((m, d), F32),
        scratch_shapes=[pltpu.VMEM((tm, d), BF16), pltpu.VMEM((tm, d), BF16),
                        pltpu.VMEM((tm, FFN_HIDDEN), BF16)],
        compiler_params=_cparams(1, VMEM_LIMIT),
        name="mix_ffn_final" if final else "mix_ffn",
    )(*args)


def _t5_bucket_ids(rel):
    half = REL_BUCKETS // 2
    max_exact = half // 2
    ret = (rel > 0).astype(jnp.int32) * half
    n = jnp.abs(rel)
    nf = jnp.maximum(n, 1).astype(F32)
    large = max_exact + (jnp.log(nf / max_exact) / math.log(REL_MAX_DIST / max_exact)
                         * (half - max_exact)).astype(jnp.int32)
    large = jnp.minimum(large, half - 1)
    return ret + jnp.where(n < max_exact, n, large)


def _bias_tiles(rel_bias, T):
    assert T >= REL_MAX_DIST
    rel0 = jnp.arange(T, dtype=jnp.int32)[:, None] - jnp.arange(T, dtype=jnp.int32)[None, :]
    ids = _t5_bucket_ids(jnp.stack([rel0, rel0 - T, rel0 - 2 * T]))[None]
    table = rel_bias.astype(F32).T * LOG2E
    tiles = jnp.zeros((table.shape[0],) + ids.shape[1:], F32)
    for bkt in range(REL_BUCKETS):
        tiles = jnp.where(ids == bkt, table[:, bkt][:, None, None, None], tiles)
    return tiles


def _hg_cumsum_lhs():
    r = np.arange(HG_BLOCK)[:, None]
    c = np.arange(HG_BLOCK)[None, :]
    same = (r // HG_STEP) == (c // HG_STEP)
    return jnp.asarray(np.concatenate([same & (c <= r), same], axis=0).astype(np.float32), BF16)


def _sb_later_matrix(T):
    s = np.arange(T)[:, None]
    j = np.arange(T)[None, :]
    later = (j > s).astype(np.float32)
    return jnp.asarray(np.concatenate([later, np.ones((SUBLANES, T), np.float32)], axis=0), BF16)


def _setup(b, s, norm_mix_g, w_in, rel_bias, diff_lambda, diff_subln_g, hgrn_lb_logits, hgrn_norm_g,
           w_up, w_out, norm_ffn_g, w_ffn_in, w_ffn_out):
    depth, d = w_in.shape[0], w_in.shape[1]
    T = min(ATT_TILE, s)
    lam_init = jnp.asarray([0.8 - 0.6 * math.exp(-0.3 * l) for l in range(depth)], F32)
    dl = diff_lambda.astype(F32)
    lam = (jnp.exp(jnp.sum(dl[:, 0] * dl[:, 1], axis=-1))
           - jnp.exp(jnp.sum(dl[:, 2] * dl[:, 3], axis=-1)) + lam_init)
    lam_rows = jnp.broadcast_to(lam[:, None, None], (depth, 1, T))
    da_g = diff_subln_g.astype(F32) * (1.0 - lam_init)[:, None]
    da_g = jnp.broadcast_to(da_g[:, :, None], (depth, LANES, T))
    lb_all = jnp.cumsum(jax.nn.softmax(hgrn_lb_logits.astype(F32), axis=0), axis=0)
    lb = (lb_all - lb_all[0:1])[:, None, :]

    col_scale = np.ones((IN_WIDTH,), np.float32)
    col_scale[COL_DA_Q * LANES:COL_DA_K * LANES] = LOG2E * DA_QK_DIM ** -0.5
    col_scale[COL_SB_Q * LANES:COL_SB_K * LANES] = LOG2E * SB_DIM ** -0.5
    w_in_s = (w_in.astype(F32) * jnp.asarray(col_scale)).astype(BF16)

    layers = dict(
        g_mix=norm_mix_g.astype(F32)[:, None, :],
        w_in=w_in_s,
        lam=lam_rows,
        da_g=da_g,
        lb=lb,
        hg_g=hgrn_norm_g.astype(F32)[:, None, :],
        w_up=w_up.astype(BF16),
        w_out=w_out.astype(BF16),
        g_ffn=norm_ffn_g.astype(F32)[:, None, :],
        w1=w_ffn_in.astype(BF16),
        w2=w_ffn_out.astype(BF16),
    )
    ones2 = jnp.asarray(np.kron(np.eye(2, dtype=np.float32), np.ones((LANES, LANES), np.float32)), BF16)
    consts = dict(bias=_bias_tiles(rel_bias, T), cs=_hg_cumsum_lhs(), ones2=ones2,
                  u=_sb_later_matrix(T), T=T)
    return layers, consts


def kernel(x, norm_mix_g, w_in, rel_bias, diff_lambda, diff_subln_g, hgrn_lb_logits, hgrn_norm_g,
           w_up, w_out, norm_ffn_g, w_ffn_in, w_ffn_out, final_norm_g):
    b, s, d = x.shape
    m = b * s
    layers, consts = _setup(b, s, norm_mix_g, w_in, rel_bias, diff_lambda, diff_subln_g,
                            hgrn_lb_logits, hgrn_norm_g, w_up, w_out, norm_ffn_g, w_ffn_in, w_ffn_out)
    T = consts["T"]

    depth = w_in.shape[0]
    x2d = x.astype(F32).reshape(m, d)
    p = layers
    for l in range(depth):
        proj = _inproj(x2d, p["g_mix"], p["w_in"], l)
        proj3 = proj.reshape(b, s, IN_WIDTH)
        y_da = _diff_attn(proj3, consts["bias"], p["lam"][l], p["da_g"][l], T)
        y_hg = _hgrn2(proj3, p["lb"][l], p["hg_g"][l], consts["cs"], consts["ones2"])
        y_sb = _stick_breaking(proj3, consts["u"], T)
        final_g = final_norm_g.astype(F32)[None, :] if l == depth - 1 else None
        x2d = _mix_ffn(y_da.reshape(m, MIX_WIDTH), y_hg.reshape(m, MIX_WIDTH),
                       y_sb.reshape(m, MIX_WIDTH), proj, p["w_up"], p["w_out"], x2d,
                       p["g_ffn"], p["w1"], p["w2"], l, final_g)
    return x2d.reshape(b, s, d).astype(x.dtype)
```
